```python
import math
import jax
import jax.numpy as jnp
from jax import lax
import numpy as np

D_MODEL = 4096
BATCH = 1
SEQ = 8192
DEPTH = 4

N_MIXERS = 3
N_A = (DEPTH + 2) // 3
N_B = (DEPTH + 1) // 3
N_C = DEPTH // 3
EPS = 1e-6
HEAD_DIM = 128
CHUNK = 128
D_SGU = D_MODEL
N_SGU_GROUPS = D_SGU // HEAD_DIM
SGU_GROUP_DIM = D_SGU // N_SGU_GROUPS
SB_HEADS = D_MODEL // HEAD_DIM
SB_BLOCK = 128
DIL_PAIRS = ((128, 1), (512, 4), (2048, 16))
N_DIL = len(DIL_PAIRS)
DIL_HEADS = 16
D_FF = ((8 * D_MODEL // 3 + 255) // 256) * 256

kernel_name = 'hybrid_sgu_stickbreak_dilated_decoder'


def rmsnorm(x, g):
    xf = x.astype(jnp.float32)
    y = xf * lax.rsqrt(jnp.mean(xf * xf, axis=-1, keepdims=True) + EPS)
    return (y * g.astype(jnp.float32)).astype(x.dtype)


def swiglu(h, w_gate, w_up, w_down):
    return (jax.nn.silu(h @ w_gate) * (h @ w_up)) @ w_down


def chunked_sgu(h, w_in, g_v, w_s, b_s, w_out):
    B, S, _ = h.shape
    z = jax.nn.gelu(h @ w_in, approximate=False)
    u, v = z[..., :D_SGU], z[..., D_SGU:]
    v = rmsnorm(v, g_v)
    vc = v.reshape(B, S // CHUNK, CHUNK, N_SGU_GROUPS, SGU_GROUP_DIM)
    w_causal = jnp.tril(w_s)
    mixed = jnp.einsum('gts,bcsgd->bctgd', w_causal, vc) + b_s.T[None, None, :, :, None]
    return (u * mixed.reshape(B, S, D_SGU)) @ w_out


def stick_breaking_attention(q, k, v):
    B, S, H, hd = q.shape
    nb = S // SB_BLOCK
    scale = 1.0 / math.sqrt(hd)
    qb = q.reshape(B, nb, SB_BLOCK, H, hd).transpose(1, 0, 2, 3, 4)
    kpos = jnp.arange(S)

    def one_block(args):
        qi, bi = args
        z = jnp.einsum('bqhd,bkhd->bhqk', qi, k).astype(jnp.float32) * scale
        qpos = bi * SB_BLOCK + jnp.arange(SB_BLOCK)
        causal = kpos[None, :] < qpos[:, None]
        log_beta = jax.nn.log_sigmoid(z)
        log_1m_beta = jnp.where(causal, jax.nn.log_sigmoid(-z), 0.0)
        log_tail = lax.cumsum(log_1m_beta, axis=log_1m_beta.ndim - 1, reverse=True) - log_1m_beta
        a = jnp.where(causal, jnp.exp(log_beta + log_tail), 0.0)
        return jnp.einsum('bhqk,bkhd->bqhd', a.astype(v.dtype), v)

    out = lax.map(one_block, (qb, jnp.arange(nb)))
    return out.transpose(1, 0, 2, 3, 4).reshape(B, S, H, hd)


def dilated_window_group(q, k, v, window, dilation):
    B, S, H, hd = q.shape
    band = window // dilation
    span = dilation * band
    L = -(-S // span) * span
    n = L // dilation
    nb = n // band
    scale = 1.0 / math.sqrt(hd)

    def to_blocks(t):
        t = jnp.pad(t, ((0, 0), (0, L - S), (0, 0), (0, 0)))
        t = t.reshape(B, n, dilation, H, hd).transpose(0, 2, 1, 3, 4)
        return t.reshape(B, dilation, nb, band, H, hd)

    def with_prev(t):
        prev = jnp.pad(t, ((0, 0), (0, 0), (1, 0), (0, 0), (0, 0), (0, 0)))[:, :, :-1]
        return jnp.concatenate([prev, t], axis=3)

    qb = to_blocks(q)
    kc = with_prev(to_blocks(k))
    vc = with_prev(to_blocks(v))
    z = jnp.einsum('brnqhd,brnkhd->brnhqk', qb, kc).astype(jnp.float32) * scale
    qi = jnp.arange(band)[:, None]
    kj = jnp.arange(2 * band)[None, :]
    dist = qi + band - kj
    blk = jnp.arange(nb)[:, None, None]
    valid = (dist >= 0) & (dist <= band) & (blk * band + kj - band >= 0)
    z = jnp.where(valid[:, None], z, -jnp.inf)
    m = jnp.max(z, axis=-1)
    p = jnp.exp(z - m[..., None])
    den = jnp.sum(p, axis=-1)
    o = jnp.einsum('brnhqk,brnkhd->brnqhd', (p / den[..., None]).astype(v.dtype), vc)
    lse = m + jnp.log(den)
    o = o.reshape(B, dilation, n, H, hd).transpose(0, 2, 1, 3, 4).reshape(B, L, H, hd)[:, :S]
    lse = lse.transpose(0, 1, 2, 4, 3).reshape(B, dilation, n, H)
    lse = lse.transpose(0, 2, 1, 3).reshape(B, L, H)[:, :S]
    return o, lse


def dilated_mixture(h, w_qkv, w_o):
    B, S, _ = h.shape
    qkv = (h @ w_qkv).reshape(B, S, N_DIL, 3, DIL_HEADS, HEAD_DIM)
    outs, lses = [], []
    for g, (window, dilation) in enumerate(DIL_PAIRS):
        o, lse = dilated_window_group(qkv[:, :, g, 0], qkv[:, :, g, 1], qkv[:, :, g, 2], window, dilation)
        outs.append(o.astype(jnp.float32))
        lses.append(lse)
    alpha = jax.nn.softmax(jnp.stack(lses, axis=0), axis=0)
    o = jnp.sum(alpha[..., None] * jnp.stack(outs, axis=0), axis=0).astype(h.dtype)
    return o.reshape(B, S, DIL_HEADS * HEAD_DIM) @ w_o


def setup_inputs(seed: int = 0) -> dict:
    key = jax.random.key(seed)
    ks = jax.random.split(key, 16)
    f32 = jnp.float32

    def nrm(k, shape, fan_in):
        return jax.random.normal(k, shape, f32) * (fan_in ** -0.5)

    def gain(k, shape):
        return 1.0 + 0.02 * jax.random.normal(k, shape, f32)

    return {
        'x': jax.random.normal(ks[0], (BATCH, SEQ, D_MODEL), f32),
        'norm_mix': gain(ks[1], (DEPTH, D_MODEL)),
        'norm_ffn': gain(ks[2], (DEPTH, D_MODEL)),
        'norm_final': gain(ks[3], (D_MODEL,)),
        'a_w_in': nrm(ks[4], (N_A, D_MODEL, 2 * D_SGU), D_MODEL),
        'a_g_v': gain(ks[5], (N_A, D_SGU)),
        'a_w_s': nrm(ks[6], (N_A, N_SGU_GROUPS, CHUNK, CHUNK), CHUNK),
        'a_b_s': gain(ks[7], (N_A, N_SGU_GROUPS, CHUNK)),
        'a_w_out': nrm(ks[8], (N_A, D_SGU, D_MODEL), D_SGU),
        'b_w_qkv': nrm(ks[9], (N_B, D_MODEL, 3 * SB_HEADS * HEAD_DIM), D_MODEL),
        'b_w_o': nrm(ks[10], (N_B, SB_HEADS * HEAD_DIM, D_MODEL), SB_HEADS * HEAD_DIM),
        'c_w_qkv': nrm(ks[11], (N_C, D_MODEL, N_DIL * 3 * DIL_HEADS * HEAD_DIM), D_MODEL),
        'c_w_o': nrm(ks[12], (N_C, DIL_HEADS * HEAD_DIM, D_MODEL), DIL_HEADS * HEAD_DIM),
        'w_gate': nrm(ks[13], (DEPTH, D_MODEL, D_FF), D_MODEL),
        'w_up': nrm(ks[14], (DEPTH, D_MODEL, D_FF), D_MODEL),
        'w_down': nrm(ks[15], (DEPTH, D_FF, D_MODEL), D_FF),
    }


def reference(x, norm_mix, norm_ffn, norm_final, a_w_in, a_g_v, a_w_s, a_b_s, a_w_out,
              b_w_qkv, b_w_o, c_w_qkv, c_w_o, w_gate, w_up, w_down):
    B, S, _ = x.shape
    h = x
    for i in range(DEPTH):
        kind, j = i % N_MIXERS, i // N_MIXERS
        hn = rmsnorm(h, norm_mix[i])
        if kind == 0:
            y = chunked_sgu(hn, a_w_in[j], a_g_v[j], a_w_s[j], a_b_s[j], a_w_out[j])
        elif kind == 1:
            qkv = (hn @ b_w_qkv[j]).reshape(B, S, 3, SB_HEADS, HEAD_DIM)
            o = stick_breaking_attention(qkv[:, :, 0], qkv[:, :, 1], qkv[:, :, 2])
            y = o.reshape(B, S, SB_HEADS * HEAD_DIM) @ b_w_o[j]
        else:
            y = dilated_mixture(hn, c_w_qkv[j], c_w_o[j])
        h = h + y
        h = h + swiglu(rmsnorm(h, norm_ffn[i]), w_gate[i], w_up[i], w_down[i])
    return rmsnorm(h, norm_final)
```

```python
import functools
import math

import jax
import jax.numpy as jnp
from jax import lax
from jax.experimental import pallas as pl
from jax.experimental.pallas import tpu as pltpu

EPS = 1e-6
LANES = 128
HEAD_DIM = 128
SGU_CHUNK = 128
DIL_PAIRS = ((128, 1), (512, 4), (2048, 16))
N_MIXERS = 3

F32 = jnp.float32
BF16 = jnp.bfloat16

VMEM_LIMIT_BYTES = 56 * 1024 * 1024


def _params(*semantics):
    return pltpu.CompilerParams(dimension_semantics=semantics, vmem_limit_bytes=VMEM_LIMIT_BYTES)


def _tile(dim, preferred):
    t = min(dim, preferred)
    while dim % t:
        t //= 2
    return t


def _rmsnorm_kernel(x_ref, g_ref, o_ref):
    x = x_ref[...]
    ms = jnp.mean(x * x, axis=-1, keepdims=True)
    o_ref[...] = (x * lax.rsqrt(ms + EPS) * g_ref[...]).astype(o_ref.dtype)


def rmsnorm(x, g, out_dtype):
    s, d = x.shape
    tr = _tile(s, 256)
    return pl.pallas_call(
        _rmsnorm_kernel,
        grid=(s // tr,),
        in_specs=[pl.BlockSpec((tr, d), lambda i: (i, 0)), pl.BlockSpec((1, d), lambda i: (0, 0))],
        out_specs=pl.BlockSpec((tr, d), lambda i: (i, 0)),
        out_shape=jax.ShapeDtypeStruct((s, d), out_dtype),
        compiler_params=_params("parallel"),
        name="rmsnorm",
    )(x, g.reshape(1, d))


def _gelu_exact(x):
    return 0.5 * x * (1.0 + lax.erf(x * (1.0 / math.sqrt(2.0))))


def _mm_kernel(x_ref, w_ref, o_ref, *, gelu):
    acc = jnp.dot(x_ref[...], w_ref[...], preferred_element_type=F32)
    if gelu:
        acc = _gelu_exact(acc)
    o_ref[...] = acc.astype(o_ref.dtype)


def matmul(x, w, out_dtype, *, gelu=False, tm=1024, tn=1024):
    m, k = x.shape
    n = w.shape[1]
    tm, tn = _tile(m, tm), _tile(n, tn)
    return pl.pallas_call(
        functools.partial(_mm_kernel, gelu=gelu),
        grid=(m // tm, n // tn),
        in_specs=[pl.BlockSpec((tm, k), lambda i, j: (i, 0)), pl.BlockSpec((k, tn), lambda i, j: (0, j))],
        out_specs=pl.BlockSpec((tm, tn), lambda i, j: (i, j)),
        out_shape=jax.ShapeDtypeStruct((m, n), out_dtype),
        compiler_params=_params("parallel", "arbitrary"),
        name="matmul_gelu" if gelu else "matmul",
    )(x, w)


def _mm_by_residue_kernel(x_ref, w_ref, o_ref, acc_ref, *, dilation):
    acc = jnp.dot(x_ref[...], w_ref[...], preferred_element_type=F32)
    rows = acc.shape[0] // dilation
    for c in range(acc_ref.shape[0]):
        lanes = slice(c * LANES, (c + 1) * LANES)
        acc_ref[c] = acc[:, lanes]
        for r in range(dilation):
            o_ref[r, :, lanes] = acc_ref[c, pl.ds(r, rows, stride=dilation), :].astype(o_ref.dtype)


def matmul_by_residue(x, w, dilation, *, tm=1024, tn=1024):
    m, k = x.shape
    n = w.shape[1]
    tm, tn = _tile(m, tm), _tile(n, tn)
    return pl.pallas_call(
        functools.partial(_mm_by_residue_kernel, dilation=dilation),
        grid=(m // tm, n // tn),
        in_specs=[pl.BlockSpec((tm, k), lambda i, j: (i, 0)), pl.BlockSpec((k, tn), lambda i, j: (0, j))],
        out_specs=pl.BlockSpec((dilation, tm // dilation, tn), lambda i, j: (0, i, j)),
        out_shape=jax.ShapeDtypeStruct((dilation, m // dilation, n), BF16),
        scratch_shapes=[pltpu.VMEM((tn // LANES, tm, LANES), F32)],
        compiler_params=_params("parallel", "arbitrary"),
        name=f"matmul_by_residue_d{dilation}",
    )(x, w)


def _mm_residual_kernel(x_ref, w_ref, r_ref, o_ref):
    acc = jnp.dot(x_ref[...], w_ref[...], preferred_element_type=F32)
    o_ref[...] = r_ref[...] + acc


def matmul_residual(x, w, res, *, tn=256):
    m, k = x.shape
    n = w.shape[1]
    tm, tn = _tile(m, 1024 if k <= 6144 else 512), _tile(n, tn)
    return pl.pallas_call(
        _mm_residual_kernel,
        grid=(m // tm, n // tn),
        in_specs=[
            pl.BlockSpec((tm, k), lambda i, j: (i, 0)),
            pl.BlockSpec((k, tn), lambda i, j: (0, j)),
            pl.BlockSpec((tm, tn), lambda i, j: (i, j)),
        ],
        out_specs=pl.BlockSpec((tm, tn), lambda i, j: (i, j)),
        out_shape=jax.ShapeDtypeStruct((m, n), F32),
        compiler_params=_params("parallel", "arbitrary"),
        name="matmul_residual",
    )(x, w, res)


def _gate_up_kernel(x_ref, wg_ref, wu_ref, o_ref):
    x = x_ref[...]
    g = jnp.dot(x, wg_ref[...], preferred_element_type=F32)
    u = jnp.dot(x, wu_ref[...], preferred_element_type=F32)
    o_ref[...] = (g * jax.nn.sigmoid(g) * u).astype(o_ref.dtype)


def gate_up(x, wg, wu, *, tm=1024, tn=256):
    m, k = x.shape
    n = wg.shape[1]
    tm, tn = _tile(m, tm), _tile(n, tn)
    return pl.pallas_call(
        _gate_up_kernel,
        grid=(m // tm, n // tn),
        in_specs=[
            pl.BlockSpec((tm, k), lambda i, j: (i, 0)),
            pl.BlockSpec((k, tn), lambda i, j: (0, j)),
            pl.BlockSpec((k, tn), lambda i, j: (0, j)),
        ],
        out_specs=pl.BlockSpec((tm, tn), lambda i, j: (i, j)),
        out_shape=jax.ShapeDtypeStruct((m, n), BF16),
        compiler_params=_params("parallel", "arbitrary"),
        name="gate_up",
    )(x, wg, wu)


def _sgu_kernel(u_ref, v_ref, gv_ref, ws_ref, bs_ref, o_ref, *, n_groups, chunks):
    v = v_ref[...]
    ms = jnp.mean(v * v, axis=-1, keepdims=True)
    vn = (v * lax.rsqrt(ms + EPS) * gv_ref[...]).astype(BF16)
    row = lax.broadcasted_iota(jnp.int32, (SGU_CHUNK, SGU_CHUNK), 0)
    col = lax.broadcasted_iota(jnp.int32, (SGU_CHUNK, SGU_CHUNK), 1)
    bs = bs_ref[...]
    for g in range(n_groups):
        w = jnp.where(col <= row, ws_ref[g], 0.0).astype(BF16)
        b = bs[:, g : g + 1]
        lanes = slice(g * HEAD_DIM, (g + 1) * HEAD_DIM)
        for c in range(chunks):
            rows = slice(c * SGU_CHUNK, (c + 1) * SGU_CHUNK)
            mixed = jnp.dot(w, vn[rows, lanes], preferred_element_type=F32) + b
            o_ref[rows, lanes] = (u_ref[rows, lanes] * mixed).astype(o_ref.dtype)


def sgu_gate(z, g_v, w_s, b_s):
    s, two_d = z.shape
    d = two_d // 2
    n_groups = d // HEAD_DIM
    chunks = 2 if s % (2 * SGU_CHUNK) == 0 else 1
    tr = chunks * SGU_CHUNK
    return pl.pallas_call(
        functools.partial(_sgu_kernel, n_groups=n_groups, chunks=chunks),
        grid=(s // tr,),
        in_specs=[
            pl.BlockSpec((tr, d), lambda i: (i, 0)),
            pl.BlockSpec((tr, d), lambda i: (i, 1)),
            pl.BlockSpec((1, d), lambda i: (0, 0)),
            pl.BlockSpec((n_groups, SGU_CHUNK, SGU_CHUNK), lambda i: (0, 0, 0)),
            pl.BlockSpec((SGU_CHUNK, n_groups), lambda i: (0, 0)),
        ],
        out_specs=pl.BlockSpec((tr, d), lambda i: (i, 0)),
        out_shape=jax.ShapeDtypeStruct((s, d), BF16),
        compiler_params=_params("parallel"),
        name="sgu_gate",
    )(z, z, g_v.reshape(1, d), w_s, b_s.T)


def _sb_kernel(q_ref, k_ref, v_ref, o_ref, *, tb, scale):
    i = pl.program_id(1)
    q = q_ref[...]
    row = lax.broadcasted_iota(jnp.int32, (tb, tb), 0)
    col = lax.broadcasted_iota(jnp.int32, (tb, tb), 1)
    later = (row > col).astype(BF16)
    causal = col < row

    def block(j, carry, acc, diagonal):
        start = pl.multiple_of(j * tb, tb)
        kb = k_ref[pl.ds(start, tb), :]
        vb = v_ref[pl.ds(start, tb), :]
        z = lax.dot_general(q, kb, (((1,), (1,)), ((), ())), preferred_element_type=F32) * scale
        softplus = jnp.maximum(z, 0.0) + jnp.log1p(jnp.exp(-jnp.abs(z)))
        log_1m_beta = -softplus
        if diagonal:
            log_1m_beta = jnp.where(causal, log_1m_beta, 0.0)
        hi = log_1m_beta.astype(BF16)
        lo = (log_1m_beta - hi.astype(F32)).astype(BF16)
        both = jnp.dot(jnp.concatenate([hi, lo], axis=0), later, preferred_element_type=F32)
        log_tail = both[:tb] + both[tb:] + carry
        a = jnp.exp((z - softplus) + log_tail)
        if diagonal:
            a = jnp.where(causal, a, 0.0)
        acc = acc + jnp.dot(a.astype(BF16), vb, preferred_element_type=F32)
        carry = carry + jnp.sum(log_1m_beta, axis=-1, keepdims=True)
        return carry, acc

    carry, acc = block(i, jnp.zeros((tb, 1), F32), jnp.zeros((tb, HEAD_DIM), F32), True)

    def body(it, state):
        return block(i - 1 - it, state[0], state[1], False)

    carry, acc = lax.fori_loop(0, i, body, (carry, acc))
    o_ref[...] = acc.astype(o_ref.dtype)


def stick_breaking(qkv, n_heads):
    s = qkv.shape[0]
    tb = _tile(s, 256)
    return pl.pallas_call(
        functools.partial(_sb_kernel, tb=tb, scale=1.0 / math.sqrt(HEAD_DIM)),
        grid=(n_heads, s // tb),
        in_specs=[
            pl.BlockSpec((tb, HEAD_DIM), lambda h, i: (i, h)),
            pl.BlockSpec((s, HEAD_DIM), lambda h, i: (0, n_heads + h)),
            pl.BlockSpec((s, HEAD_DIM), lambda h, i: (0, 2 * n_heads + h)),
        ],
        out_specs=pl.BlockSpec((tb, HEAD_DIM), lambda h, i: (i, h)),
        out_shape=jax.ShapeDtypeStruct((s, n_heads * HEAD_DIM), BF16),
        compiler_params=_params("parallel", "arbitrary"),
        name="stick_breaking",
    )(qkv, qkv, qkv)


def _dilated_kernel(q_ref, kp_ref, kc_ref, vp_ref, vc_ref, o_ref, lse_ref, *, n_heads, band, scale):
    has_prev = pl.program_id(1) > 0
    qi = lax.broadcasted_iota(jnp.int32, (band, band), 0)
    kj = lax.broadcasted_iota(jnp.int32, (band, band), 1)
    valid_prev = (kj >= qi) & has_prev
    valid_cur = kj <= qi
    lane = lax.broadcasted_iota(jnp.int32, (band, LANES), 1)
    dims = (((1,), (1,)), ((), ()))
    lse_all = jnp.zeros((band, LANES), F32)
    for h in range(n_heads):
        lanes = slice(h * HEAD_DIM, (h + 1) * HEAD_DIM)
        q = q_ref[:, lanes]
        zp = lax.dot_general(q, kp_ref[:, lanes], dims, preferred_element_type=F32) * scale
        zc = lax.dot_general(q, kc_ref[:, lanes], dims, preferred_element_type=F32) * scale
        zp = jnp.where(valid_prev, zp, -jnp.inf)
        zc = jnp.where(valid_cur, zc, -jnp.inf)
        m = jnp.maximum(jnp.max(zp, axis=-1, keepdims=True), jnp.max(zc, axis=-1, keepdims=True))
        pp = jnp.exp(zp - m)
        pc = jnp.exp(zc - m)
        den = jnp.sum(pp, axis=-1, keepdims=True) + jnp.sum(pc, axis=-1, keepdims=True)
        o = jnp.dot((pp / den).astype(BF16), vp_ref[:, lanes], preferred_element_type=F32)
        o = o + jnp.dot((pc / den).astype(BF16), vc_ref[:, lanes], preferred_element_type=F32)
        o_ref[:, lanes] = o
        lse_all = jnp.where(lane == h, m + jnp.log(den), lse_all)
    lse_ref[...] = lse_all


def dilated_group(qkv, n_heads, band):
    dilation, n, _ = qkv.shape
    assert n_heads <= LANES
    width = n_heads * HEAD_DIM
    blk = (None, band, width)
    cur = lambda which: pl.BlockSpec(blk, lambda r, b: (r, b, which))
    prev = lambda which: pl.BlockSpec(blk, lambda r, b: (r, jnp.maximum(b - 1, 0), which))
    return pl.pallas_call(
        functools.partial(_dilated_kernel, n_heads=n_heads, band=band, scale=1.0 / math.sqrt(HEAD_DIM)),
        grid=(dilation, n // band),
        in_specs=[cur(0), prev(1), cur(1), prev(2), cur(2)],
        out_specs=[
            pl.BlockSpec(blk, lambda r, b: (r, b, 0)),
            pl.BlockSpec((None, band, LANES), lambda r, b: (r, b, 0)),
        ],
        out_shape=[
            jax.ShapeDtypeStruct((dilation, n, width), F32),
            jax.ShapeDtypeStruct((dilation, n, LANES), F32),
        ],
        compiler_params=_params("parallel", "arbitrary"),
        name=f"dilated_d{dilation}",
    )(qkv, qkv, qkv, qkv, qkv)


def _mix_kernel(*refs, dilations, n_heads):
    n_groups = len(dilations)
    o_refs, lse_refs = refs[:n_groups], refs[n_groups : 2 * n_groups]
    out_ref, o_seq, lse_seq = refs[2 * n_groups :]
    tr = out_ref.shape[0]
    for g, d in enumerate(dilations):
        for r in range(d):
            rows = pl.ds(r, tr // d, stride=d)
            lse_seq[g, rows, :] = lse_refs[g][r]
            for h in range(n_heads):
                o_seq[g, h, rows, :] = o_refs[g][r, :, h * HEAD_DIM : (h + 1) * HEAD_DIM]
    lses = [lse_seq[g] for g in range(n_groups)]
    m = functools.reduce(jnp.maximum, lses)
    es = [jnp.exp(l - m) for l in lses]
    total = functools.reduce(jnp.add, es)
    alphas = [e / total for e in es]
    for h in range(n_heads):
        acc = alphas[0][:, h : h + 1] * o_seq[0, h]
        for g in range(1, n_groups):
            acc = acc + alphas[g][:, h : h + 1] * o_seq[g, h]
        out_ref[:, h * HEAD_DIM : (h + 1) * HEAD_DIM] = acc.astype(out_ref.dtype)


def mix_groups(outs, lses):
    n_groups = len(outs)
    dilations = tuple(o.shape[0] for o in outs)
    width = outs[0].shape[2]
    n_heads = width // HEAD_DIM
    s = outs[0].shape[0] * outs[0].shape[1]
    tr = _tile(s, 256)
    return pl.pallas_call(
        functools.partial(_mix_kernel, dilations=dilations, n_heads=n_heads),
        grid=(s // tr,),
        in_specs=[pl.BlockSpec((d, tr // d, width), lambda i: (0, i, 0)) for d in dilations]
        + [pl.BlockSpec((d, tr // d, LANES), lambda i: (0, i, 0)) for d in dilations],
        out_specs=pl.BlockSpec((tr, width), lambda i: (i, 0)),
        out_shape=jax.ShapeDtypeStruct((s, width), BF16),
        scratch_shapes=[pltpu.VMEM((n_groups, n_heads, tr, HEAD_DIM), F32), pltpu.VMEM((n_groups, tr, LANES), F32)],
        compiler_params=_params("parallel"),
        name="mix_groups",
    )(*outs, *lses)


def kernel(x, norm_mix, norm_ffn, norm_final, a_w_in, a_g_v, a_w_s, a_b_s, a_w_out,
           b_w_qkv, b_w_o, c_w_qkv, c_w_o, w_gate, w_up, w_down):
    batch, seq, d_model = x.shape
    depth = norm_mix.shape[0]
    n_dil = len(DIL_PAIRS)
    outs = []
    for b in range(batch):
        h = x[b]
        for i in range(depth):
            kind, j = i % N_MIXERS, i // N_MIXERS
            hn = rmsnorm(h, norm_mix[i], BF16)
            if kind == 0:
                z = matmul(hn, a_w_in[j].astype(BF16), F32, gelu=True)
                gated = sgu_gate(z, a_g_v[j], a_w_s[j], a_b_s[j])
                h = matmul_residual(gated, a_w_out[j].astype(BF16), h)
            elif kind == 1:
                n_heads = b_w_o.shape[1] // HEAD_DIM
                qkv = matmul(hn, b_w_qkv[j].astype(BF16), BF16)
                o = stick_breaking(qkv, n_heads)
                h = matmul_residual(o, b_w_o[j].astype(BF16), h)
            else:
                n_heads = c_w_o.shape[1] // HEAD_DIM
                w_groups = c_w_qkv[j].astype(BF16).reshape(d_model, n_dil, 3 * n_heads * HEAD_DIM)
                parts = []
                for g, (window, dilation) in enumerate(DIL_PAIRS):
                    qkv = matmul_by_residue(hn, w_groups[:, g], dilation)
                    parts.append(dilated_group(qkv, n_heads, window // dilation))
                o = mix_groups([p[0] for p in parts], [p[1] for p in parts])
                h = matmul_residual(o, c_w_o[j].astype(BF16), h)
            hf = rmsnorm(h, norm_ffn[i], BF16)
            act = gate_up(hf, w_gate[i].astype(BF16), w_up[i].astype(BF16))
            h = matmul_residual(act, w_down[i].astype(BF16), h)
        outs.append(rmsnorm(h, norm_final, F32))
    return jnp.stack(outs, axis=0)
```

```python
import functools
import math

import jax
import jax.numpy as jnp
from jax import lax
from jax.experimental import pallas as pl
from jax.experimental.pallas import tpu as pltpu

EPS = 1e-6
LANES = 128
HEAD_DIM = 128
SGU_CHUNK = 128
DIL_PAIRS = ((128, 1), (512, 4), (2048, 16))
N_MIXERS = 3

LOG2_E = 1.4426950408889634

F32 = jnp.float32
BF16 = jnp.bfloat16

VMEM_LIMIT_BYTES = 56 * 1024 * 1024


def _params(*semantics):
    return pltpu.CompilerParams(dimension_semantics=semantics, vmem_limit_bytes=VMEM_LIMIT_BYTES)


def _tile(dim, preferred):
    t = min(dim, preferred)
    while dim % t:
        t //= 2
    return t


def _rmsnorm_kernel(x_ref, g_ref, o_ref):
    x = x_ref[...]
    ms = jnp.mean(x * x, axis=-1, keepdims=True)
    o_ref[...] = (x * lax.rsqrt(ms + EPS) * g_ref[...]).astype(o_ref.dtype)


def rmsnorm(x, g, out_dtype):
    s, d = x.shape
    tr = _tile(s, 256)
    return pl.pallas_call(
        _rmsnorm_kernel,
        grid=(s // tr,),
        in_specs=[pl.BlockSpec((tr, d), lambda i: (i, 0)), pl.BlockSpec((1, d), lambda i: (0, 0))],
        out_specs=pl.BlockSpec((tr, d), lambda i: (i, 0)),
        out_shape=jax.ShapeDtypeStruct((s, d), out_dtype),
        compiler_params=_params("parallel"),
        name="rmsnorm",
    )(x, g.reshape(1, d))


def _gelu_exact(x):
    return 0.5 * x * (1.0 + lax.erf(x * (1.0 / math.sqrt(2.0))))


def _mm_kernel(x_ref, w_ref, o_ref, *, gelu):
    acc = jnp.dot(x_ref[...], w_ref[...], preferred_element_type=F32)
    if gelu:
        acc = _gelu_exact(acc)
    o_ref[...] = acc.astype(o_ref.dtype)


def _weight_spec(k, tn, layer, col_block0=0):
    return pl.BlockSpec((None, k, tn), lambda i, j: (layer, 0, col_block0 + j))


def matmul(x, w, layer, out_dtype, *, gelu=False, tm=1024, tn=1024):
    m, k = x.shape
    n = w.shape[2]
    tm, tn = _tile(m, tm), _tile(n, tn)
    return pl.pallas_call(
        functools.partial(_mm_kernel, gelu=gelu),
        grid=(m // tm, n // tn),
        in_specs=[pl.BlockSpec((tm, k), lambda i, j: (i, 0)), _weight_spec(k, tn, layer)],
        out_specs=pl.BlockSpec((tm, tn), lambda i, j: (i, j)),
        out_shape=jax.ShapeDtypeStruct((m, n), out_dtype),
        compiler_params=_params("parallel", "arbitrary"),
        name="matmul_gelu" if gelu else "matmul",
    )(x, w)


def _mm_by_residue_kernel(x_ref, w_ref, o_ref, acc_ref, *, dilation):
    acc = jnp.dot(x_ref[...], w_ref[...], preferred_element_type=F32)
    rows = acc.shape[0] // dilation
    for c in range(acc_ref.shape[0]):
        lanes = slice(c * LANES, (c + 1) * LANES)
        acc_ref[c] = acc[:, lanes]
        for r in range(dilation):
            o_ref[r, :, lanes] = acc_ref[c, pl.ds(r, rows, stride=dilation), :].astype(o_ref.dtype)


def matmul_by_residue(x, w, layer, group, n_groups, dilation, *, tm=1024, tn=1024):
    m, k = x.shape
    n = w.shape[2] // n_groups
    tm, tn = _tile(m, tm), _tile(n, tn)
    return pl.pallas_call(
        functools.partial(_mm_by_residue_kernel, dilation=dilation),
        grid=(m // tm, n // tn),
        in_specs=[pl.BlockSpec((tm, k), lambda i, j: (i, 0)), _weight_spec(k, tn, layer, group * (n // tn))],
        out_specs=pl.BlockSpec((dilation, tm // dilation, tn), lambda i, j: (0, i, j)),
        out_shape=jax.ShapeDtypeStruct((dilation, m // dilation, n), BF16),
        scratch_shapes=[pltpu.VMEM((tn // LANES, tm, LANES), F32)],
        compiler_params=_params("parallel", "arbitrary"),
        name=f"matmul_by_residue_d{dilation}",
    )(x, w)


def _mm_residual_kernel(x_ref, w_ref, r_ref, o_ref):
    acc = jnp.dot(x_ref[...], w_ref[...], preferred_element_type=F32)
    o_ref[...] = r_ref[...] + acc


def matmul_residual(x, w, layer, res, *, tn=256):
    m, k = x.shape
    n = w.shape[2]
    tm, tn = _tile(m, 1024 if k <= 6144 else 512), _tile(n, tn)
    return pl.pallas_call(
        _mm_residual_kernel,
        grid=(m // tm, n // tn),
        in_specs=[
            pl.BlockSpec((tm, k), lambda i, j: (i, 0)),
            _weight_spec(k, tn, layer),
            pl.BlockSpec((tm, tn), lambda i, j: (i, j)),
        ],
        out_specs=pl.BlockSpec((tm, tn), lambda i, j: (i, j)),
        out_shape=jax.ShapeDtypeStruct((m, n), F32),
        compiler_params=_params("parallel", "arbitrary"),
        name="matmul_residual",
    )(x, w, res)


def _gate_up_kernel(x_ref, wg_ref, wu_ref, o_ref):
    x = x_ref[...]
    g = jnp.dot(x, wg_ref[...], preferred_element_type=F32)
    u = jnp.dot(x, wu_ref[...], preferred_element_type=F32)
    o_ref[...] = (g * jax.nn.sigmoid(g) * u).astype(o_ref.dtype)


def gate_up(x, wg, wu, layer, *, tm=1024, tn=256):
    m, k = x.shape
    n = wg.shape[2]
    tm, tn = _tile(m, tm), _tile(n, tn)
    return pl.pallas_call(
        _gate_up_kernel,
        grid=(m // tm, n // tn),
        in_specs=[
            pl.BlockSpec((tm, k), lambda i, j: (i, 0)),
            _weight_spec(k, tn, layer),
            _weight_spec(k, tn, layer),
        ],
        out_specs=pl.BlockSpec((tm, tn), lambda i, j: (i, j)),
        out_shape=jax.ShapeDtypeStruct((m, n), BF16),
        compiler_params=_params("parallel", "arbitrary"),
        name="gate_up",
    )(x, wg, wu)


def _sgu_kernel(u_ref, v_ref, gv_ref, ws_ref, bs_ref, o_ref, *, n_groups, chunks):
    v = v_ref[...]
    ms = jnp.mean(v * v, axis=-1, keepdims=True)
    vn = (v * lax.rsqrt(ms + EPS) * gv_ref[...]).astype(BF16)
    row = lax.broadcasted_iota(jnp.int32, (SGU_CHUNK, SGU_CHUNK), 0)
    col = lax.broadcasted_iota(jnp.int32, (SGU_CHUNK, SGU_CHUNK), 1)
    bs = bs_ref[...]
    for g in range(n_groups):
        w = jnp.where(col <= row, ws_ref[g], 0.0).astype(BF16)
        b = bs[:, g : g + 1]
        lanes = slice(g * HEAD_DIM, (g + 1) * HEAD_DIM)
        for c in range(chunks):
            rows = slice(c * SGU_CHUNK, (c + 1) * SGU_CHUNK)
            mixed = jnp.dot(w, vn[rows, lanes], preferred_element_type=F32) + b
            o_ref[rows, lanes] = (u_ref[rows, lanes] * mixed).astype(o_ref.dtype)


def sgu_gate(z, g_v, w_s, b_s):
    s, two_d = z.shape
    d = two_d // 2
    n_groups = d // HEAD_DIM
    chunks = 2 if s % (2 * SGU_CHUNK) == 0 else 1
    tr = chunks * SGU_CHUNK
    return pl.pallas_call(
        functools.partial(_sgu_kernel, n_groups=n_groups, chunks=chunks),
        grid=(s // tr,),
        in_specs=[
            pl.BlockSpec((tr, d), lambda i: (i, 0)),
            pl.BlockSpec((tr, d), lambda i: (i, 1)),
            pl.BlockSpec((1, d), lambda i: (0, 0)),
            pl.BlockSpec((n_groups, SGU_CHUNK, SGU_CHUNK), lambda i: (0, 0, 0)),
            pl.BlockSpec((SGU_CHUNK, n_groups), lambda i: (0, 0)),
        ],
        out_specs=pl.BlockSpec((tr, d), lambda i: (i, 0)),
        out_shape=jax.ShapeDtypeStruct((s, d), BF16),
        compiler_params=_params("parallel"),
        name="sgu_gate",
    )(z, z, g_v.reshape(1, d), w_s, b_s.T)


def _sb_kernel(q_ref, k_ref, v_ref, o_ref, acc_ref, *, tb, heads, scale_log2):
    i = pl.program_id(1)
    row = lax.broadcasted_iota(jnp.int32, (tb, tb), 0)
    col = lax.broadcasted_iota(jnp.int32, (tb, tb), 1)
    neg_later = jnp.where(row > col, -1.0, 0.0).astype(BF16)
    causal = col < row
    dims = (((1,), (1,)), ((), ()))

    def block(j, carries, diagonal):
        start = pl.multiple_of(j * tb, tb)
        chains = range(heads)
        lanes = [slice(c * HEAD_DIM, (c + 1) * HEAD_DIM) for c in chains]
        z2 = [lax.dot_general(q_ref[:, lanes[c]], k_ref[pl.ds(start, tb), lanes[c]], dims,
                              preferred_element_type=F32) * scale_log2 for c in chains]
        log2_beta, parts, new_carries = [], [], []
        for c in chains:
            sp2 = jnp.maximum(z2[c], 0.0) + jnp.log(1.0 + jnp.exp2(-jnp.abs(z2[c]))) * LOG2_E
            log2_beta.append(z2[c] - sp2)
            if diagonal:
                sp2 = jnp.where(causal, sp2, 0.0)
            hi = pltpu.bitcast(pltpu.bitcast(sp2, jnp.uint32) & jnp.uint32(0xFFFF0000), F32)
            parts.append(jnp.concatenate([hi.astype(BF16), (sp2 - hi).astype(BF16)], axis=0))
            new_carries.append(carries[c] - jnp.sum(sp2, axis=-1, keepdims=True))
        both = [jnp.dot(parts[c], neg_later, preferred_element_type=F32) for c in chains]
        a = []
        for c in chains:
            log2_tail = both[c][:tb] + both[c][tb:] + carries[c]
            a_c = jnp.exp2(log2_beta[c] + log2_tail)
            if diagonal:
                a_c = jnp.where(causal, a_c, 0.0)
            a.append(a_c.astype(BF16))
        for c in chains:
            av = jnp.dot(a[c], v_ref[pl.ds(start, tb), lanes[c]], preferred_element_type=F32)
            if diagonal:
                acc_ref[c] = av
            else:
                acc_ref[c] += av
        return tuple(new_carries)

    carries = block(i, tuple(jnp.zeros((tb, 1), F32) for _ in range(heads)), True)
    lax.fori_loop(0, i, lambda it, carries: block(i - 1 - it, carries, False), carries)
    for c in range(heads):
        o_ref[:, c * HEAD_DIM : (c + 1) * HEAD_DIM] = acc_ref[c].astype(o_ref.dtype)


def stick_breaking(qkv, n_heads):
    s = qkv.shape[0]
    tb = _tile(s, 256)
    heads = _tile(n_heads, 4)
    groups = n_heads // heads
    width = heads * HEAD_DIM
    return pl.pallas_call(
        functools.partial(_sb_kernel, tb=tb, heads=heads, scale_log2=LOG2_E / math.sqrt(HEAD_DIM)),
        grid=(groups, s // tb),
        in_specs=[
            pl.BlockSpec((tb, width), lambda h, i: (i, h)),
            pl.BlockSpec((s, width), lambda h, i: (0, groups + h)),
            pl.BlockSpec((s, width), lambda h, i: (0, 2 * groups + h)),
        ],
        out_specs=pl.BlockSpec((tb, width), lambda h, i: (i, h)),
        out_shape=jax.ShapeDtypeStruct((s, n_heads * HEAD_DIM), BF16),
        scratch_shapes=[pltpu.VMEM((heads, tb, HEAD_DIM), F32)],
        compiler_params=_params("parallel", "arbitrary"),
        name="stick_breaking",
    )(qkv, qkv, qkv)


def _dilated_kernel(q_ref, kp_ref, kc_ref, vp_ref, vc_ref, o_ref, lse_ref, *, n_heads, band, scale):
    has_prev = pl.program_id(1) > 0
    qi = lax.broadcasted_iota(jnp.int32, (band, band), 0)
    kj = lax.broadcasted_iota(jnp.int32, (band, band), 1)
    valid_prev = (kj >= qi) & has_prev
    valid_cur = kj <= qi
    lane = lax.broadcasted_iota(jnp.int32, (band, LANES), 1)
    dims = (((1,), (1,)), ((), ()))
    lse_all = jnp.zeros((band, LANES), F32)
    for h in range(n_heads):
        lanes = slice(h * HEAD_DIM, (h + 1) * HEAD_DIM)
        q = q_ref[:, lanes]
        zp = lax.dot_general(q, kp_ref[:, lanes], dims, preferred_element_type=F32) * scale
        zc = lax.dot_general(q, kc_ref[:, lanes], dims, preferred_element_type=F32) * scale
        zp = jnp.where(valid_prev, zp, -jnp.inf)
        zc = jnp.where(valid_cur, zc, -jnp.inf)
        m = jnp.maximum(jnp.max(zp, axis=-1, keepdims=True), jnp.max(zc, axis=-1, keepdims=True))
        pp = jnp.exp(zp - m)
        pc = jnp.exp(zc - m)
        den = jnp.sum(pp, axis=-1, keepdims=True) + jnp.sum(pc, axis=-1, keepdims=True)
        o = jnp.dot((pp / den).astype(BF16), vp_ref[:, lanes], preferred_element_type=F32)
        o = o + jnp.dot((pc / den).astype(BF16), vc_ref[:, lanes], preferred_element_type=F32)
        o_ref[:, lanes] = o
        lse_all = jnp.where(lane == h, m + jnp.log(den), lse_all)
    lse_ref[...] = lse_all


def dilated_group(qkv, n_heads, band):
    dilation, n, _ = qkv.shape
    assert n_heads <= LANES
    width = n_heads * HEAD_DIM
    blk = (None, band, width)
    cur = lambda which: pl.BlockSpec(blk, lambda r, b: (r, b, which))
    prev = lambda which: pl.BlockSpec(blk, lambda r, b: (r, jnp.maximum(b - 1, 0), which))
    return pl.pallas_call(
        functools.partial(_dilated_kernel, n_heads=n_heads, band=band, scale=1.0 / math.sqrt(HEAD_DIM)),
        grid=(dilation, n // band),
        in_specs=[cur(0), prev(1), cur(1), prev(2), cur(2)],
        out_specs=[
            pl.BlockSpec(blk, lambda r, b: (r, b, 0)),
            pl.BlockSpec((None, band, LANES), lambda r, b: (r, b, 0)),
        ],
        out_shape=[
            jax.ShapeDtypeStruct((dilation, n, width), F32),
            jax.ShapeDtypeStruct((dilation, n, LANES), F32),
        ],
        compiler_params=_params("parallel", "arbitrary"),
        name=f"dilated_d{dilation}",
    )(qkv, qkv, qkv, qkv, qkv)


def _mix_kernel(*refs, dilations, n_heads):
    n_groups = len(dilations)
    o_refs, lse_refs = refs[:n_groups], refs[n_groups : 2 * n_groups]
    out_ref, o_seq, lse_seq = refs[2 * n_groups :]
    tr = out_ref.shape[0]
    for g, d in enumerate(dilations):
        for r in range(d):
            rows = pl.ds(r, tr // d, stride=d)
            lse_seq[g, rows, :] = lse_refs[g][r]
            for h in range(n_heads):
                o_seq[g, h, rows, :] = o_refs[g][r, :, h * HEAD_DIM : (h + 1) * HEAD_DIM]
    lses = [lse_seq[g] for g in range(n_groups)]
    m = functools.reduce(jnp.maximum, lses)
    es = [jnp.exp(l - m) for l in lses]
    total = functools.reduce(jnp.add, es)
    alphas = [e / total for e in es]
    for h in range(n_heads):
        acc = alphas[0][:, h : h + 1] * o_seq[0, h]
        for g in range(1, n_groups):
            acc = acc + alphas[g][:, h : h + 1] * o_seq[g, h]
        out_ref[:, h * HEAD_DIM : (h + 1) * HEAD_DIM] = acc.astype(out_ref.dtype)


def mix_groups(outs, lses):
    n_groups = len(outs)
    dilations = tuple(o.shape[0] for o in outs)
    width = outs[0].shape[2]
    n_heads = width // HEAD_DIM
    s = outs[0].shape[0] * outs[0].shape[1]
    tr = _tile(s, 256)
    return pl.pallas_call(
        functools.partial(_mix_kernel, dilations=dilations, n_heads=n_heads),
        grid=(s // tr,),
        in_specs=[pl.BlockSpec((d, tr // d, width), lambda i: (0, i, 0)) for d in dilations]
        + [pl.BlockSpec((d, tr // d, LANES), lambda i: (0, i, 0)) for d in dilations],
        out_specs=pl.BlockSpec((tr, width), lambda i: (i, 0)),
        out_shape=jax.ShapeDtypeStruct((s, width), BF16),
        scratch_shapes=[pltpu.VMEM((n_groups, n_heads, tr, HEAD_DIM), F32), pltpu.VMEM((n_groups, tr, LANES), F32)],
        compiler_params=_params("parallel"),
        name="mix_groups",
    )(*outs, *lses)


def kernel(x, norm_mix, norm_ffn, norm_final, a_w_in, a_g_v, a_w_s, a_b_s, a_w_out,
           b_w_qkv, b_w_o, c_w_qkv, c_w_o, w_gate, w_up, w_down):
    batch, seq, d_model = x.shape
    depth = norm_mix.shape[0]
    n_dil = len(DIL_PAIRS)
    a_w_in, a_w_out, b_w_qkv, b_w_o, c_w_qkv, c_w_o, w_gate, w_up, w_down = (
        w.astype(BF16) for w in (a_w_in, a_w_out, b_w_qkv, b_w_o, c_w_qkv, c_w_o, w_gate, w_up, w_down))
    outs = []
    for b in range(batch):
        h = x[b]
        for i in range(depth):
            kind, j = i % N_MIXERS, i // N_MIXERS
            hn = rmsnorm(h, norm_mix[i], BF16)
            if kind == 0:
                z = matmul(hn, a_w_in, j, F32, gelu=True)
                gated = sgu_gate(z, a_g_v[j], a_w_s[j], a_b_s[j])
                h = matmul_residual(gated, a_w_out, j, h)
            elif kind == 1:
                n_heads = b_w_o.shape[1] // HEAD_DIM
                qkv = matmul(hn, b_w_qkv, j, BF16)
                o = stick_breaking(qkv, n_heads)
                h = matmul_residual(o, b_w_o, j, h)
            else:
                n_heads = c_w_o.shape[1] // HEAD_DIM
                parts = []
                for g, (window, dilation) in enumerate(DIL_PAIRS):
                    qkv = matmul_by_residue(hn, c_w_qkv, j, g, n_dil, dilation)
                    parts.append(dilated_group(qkv, n_heads, window // dilation))
                o = mix_groups([p[0] for p in parts], [p[1] for p in parts])
                h = matmul_residual(o, c_w_o, j, h)
            hf = rmsnorm(h, norm_ffn[i], BF16)
            act = gate_up(hf, w_gate, w_up, i)
            h = matmul_residual(act, w_down, i, h)
        outs.append(rmsnorm(h, norm_final, F32))
    return jnp.stack(outs, axis=0)
```

```python
import functools
import math

import jax
import jax.numpy as jnp
from jax import lax
from jax.experimental import pallas as pl
from jax.experimental.pallas import tpu as pltpu

EPS = 1e-6
LANES = 128
HEAD_DIM = 128
SGU_CHUNK = 128
DIL_PAIRS = ((128, 1), (512, 4), (2048, 16))
N_MIXERS = 3

LOG2_E = 1.4426950408889634
EXP2_TO_ZERO = -151.0

F32 = jnp.float32
BF16 = jnp.bfloat16

VMEM_LIMIT_BYTES = 56 * 1024 * 1024


def _params(*semantics):
    return pltpu.CompilerParams(dimension_semantics=semantics, vmem_limit_bytes=VMEM_LIMIT_BYTES)


def _tile(dim, preferred):
    t = min(dim, preferred)
    while dim % t:
        t //= 2
    return t


def _rmsnorm_kernel(x_ref, g_ref, o_ref):
    x = x_ref[...]
    ms = jnp.mean(x * x, axis=-1, keepdims=True)
    o_ref[...] = (x * lax.rsqrt(ms + EPS) * g_ref[...]).astype(o_ref.dtype)


def rmsnorm(x, g, out_dtype):
    s, d = x.shape
    tr = _tile(s, 256)
    return pl.pallas_call(
        _rmsnorm_kernel,
        grid=(s // tr,),
        in_specs=[pl.BlockSpec((tr, d), lambda i: (i, 0)), pl.BlockSpec((1, d), lambda i: (0, 0))],
        out_specs=pl.BlockSpec((tr, d), lambda i: (i, 0)),
        out_shape=jax.ShapeDtypeStruct((s, d), out_dtype),
        compiler_params=_params("parallel"),
        name="rmsnorm",
    )(x, g.reshape(1, d))


def _gelu_exact(x):
    return 0.5 * x * (1.0 + lax.erf(x * (1.0 / math.sqrt(2.0))))


def _mm_kernel(x_ref, w_ref, o_ref, *, gelu):
    acc = jnp.dot(x_ref[...], w_ref[...], preferred_element_type=F32)
    if gelu:
        acc = _gelu_exact(acc)
    o_ref[...] = acc.astype(o_ref.dtype)


def _weight_spec(k, tn, layer, col_block0=0):
    return pl.BlockSpec((None, k, tn), lambda i, j: (layer, 0, col_block0 + j))


def matmul(x, w, layer, out_dtype, *, gelu=False, tm=1024, tn=1024):
    m, k = x.shape
    n = w.shape[2]
    tm, tn = _tile(m, tm), _tile(n, tn)
    return pl.pallas_call(
        functools.partial(_mm_kernel, gelu=gelu),
        grid=(m // tm, n // tn),
        in_specs=[pl.BlockSpec((tm, k), lambda i, j: (i, 0)), _weight_spec(k, tn, layer)],
        out_specs=pl.BlockSpec((tm, tn), lambda i, j: (i, j)),
        out_shape=jax.ShapeDtypeStruct((m, n), out_dtype),
        compiler_params=_params("parallel", "arbitrary"),
        name="matmul_gelu" if gelu else "matmul",
    )(x, w)


def _mm_by_residue_kernel(x_ref, w_ref, o_ref, acc_ref, *, dilation):
    acc = jnp.dot(x_ref[...], w_ref[...], preferred_element_type=F32)
    rows = acc.shape[0] // dilation
    for c in range(acc_ref.shape[0]):
        lanes = slice(c * LANES, (c + 1) * LANES)
        acc_ref[c] = acc[:, lanes]
        for r in range(dilation):
            o_ref[r, :, lanes] = acc_ref[c, pl.ds(r, rows, stride=dilation), :].astype(o_ref.dtype)


def matmul_by_residue(x, w, layer, group, n_groups, dilation, *, tm=1024, tn=1024):
    m, k = x.shape
    n = w.shape[2] // n_groups
    tm, tn = _tile(m, tm), _tile(n, tn)
    return pl.pallas_call(
        functools.partial(_mm_by_residue_kernel, dilation=dilation),
        grid=(m // tm, n // tn),
        in_specs=[pl.BlockSpec((tm, k), lambda i, j: (i, 0)), _weight_spec(k, tn, layer, group * (n // tn))],
        out_specs=pl.BlockSpec((dilation, tm // dilation, tn), lambda i, j: (0, i, j)),
        out_shape=jax.ShapeDtypeStruct((dilation, m // dilation, n), BF16),
        scratch_shapes=[pltpu.VMEM((tn // LANES, tm, LANES), F32)],
        compiler_params=_params("parallel", "arbitrary"),
        name=f"matmul_by_residue_d{dilation}",
    )(x, w)


def _mm_residual_kernel(x_ref, w_ref, r_ref, o_ref):
    acc = jnp.dot(x_ref[...], w_ref[...], preferred_element_type=F32)
    o_ref[...] = r_ref[...] + acc


def matmul_residual(x, w, layer, res):
    m, k = x.shape
    n = w.shape[2]
    tm, tn = (_tile(m, 1024), _tile(n, 1024)) if k <= 4096 else (_tile(m, 512), _tile(n, 512))
    return pl.pallas_call(
        _mm_residual_kernel,
        grid=(m // tm, n // tn),
        in_specs=[
            pl.BlockSpec((tm, k), lambda i, j: (i, 0)),
            _weight_spec(k, tn, layer),
            pl.BlockSpec((tm, tn), lambda i, j: (i, j)),
        ],
        out_specs=pl.BlockSpec((tm, tn), lambda i, j: (i, j)),
        out_shape=jax.ShapeDtypeStruct((m, n), F32),
        compiler_params=_params("parallel", "arbitrary"),
        name="matmul_residual",
    )(x, w, res)


def _gate_up_kernel(x_ref, wg_ref, wu_ref, o_ref):
    x = x_ref[...]
    g = jnp.dot(x, wg_ref[...], preferred_element_type=F32)
    u = jnp.dot(x, wu_ref[...], preferred_element_type=F32)
    o_ref[...] = (g * jax.nn.sigmoid(g) * u).astype(o_ref.dtype)


def gate_up(x, wg, wu, layer, *, tm=1024, tn=256):
    m, k = x.shape
    n = wg.shape[2]
    tm, tn = _tile(m, tm), _tile(n, tn)
    return pl.pallas_call(
        _gate_up_kernel,
        grid=(m // tm, n // tn),
        in_specs=[
            pl.BlockSpec((tm, k), lambda i, j: (i, 0)),
            _weight_spec(k, tn, layer),
            _weight_spec(k, tn, layer),
        ],
        out_specs=pl.BlockSpec((tm, tn), lambda i, j: (i, j)),
        out_shape=jax.ShapeDtypeStruct((m, n), BF16),
        compiler_params=_params("parallel", "arbitrary"),
        name="gate_up",
    )(x, wg, wu)


def _sgu_kernel(u_ref, v_ref, gv_ref, ws_ref, bs_ref, o_ref, *, n_groups, chunks):
    v = v_ref[...]
    ms = jnp.mean(v * v, axis=-1, keepdims=True)
    vn = (v * lax.rsqrt(ms + EPS) * gv_ref[...]).astype(BF16)
    row = lax.broadcasted_iota(jnp.int32, (SGU_CHUNK, SGU_CHUNK), 0)
    col = lax.broadcasted_iota(jnp.int32, (SGU_CHUNK, SGU_CHUNK), 1)
    bs = bs_ref[...]
    for g in range(n_groups):
        w = jnp.where(col <= row, ws_ref[g], 0.0).astype(BF16)
        b = bs[:, g : g + 1]
        lanes = slice(g * HEAD_DIM, (g + 1) * HEAD_DIM)
        for c in range(chunks):
            rows = slice(c * SGU_CHUNK, (c + 1) * SGU_CHUNK)
            mixed = jnp.dot(w, vn[rows, lanes], preferred_element_type=F32) + b
            o_ref[rows, lanes] = (u_ref[rows, lanes] * mixed).astype(o_ref.dtype)


def sgu_gate(z, g_v, w_s, b_s):
    s, two_d = z.shape
    d = two_d // 2
    n_groups = d // HEAD_DIM
    chunks = 2 if s % (2 * SGU_CHUNK) == 0 else 1
    tr = chunks * SGU_CHUNK
    return pl.pallas_call(
        functools.partial(_sgu_kernel, n_groups=n_groups, chunks=chunks),
        grid=(s // tr,),
        in_specs=[
            pl.BlockSpec((tr, d), lambda i: (i, 0)),
            pl.BlockSpec((tr, d), lambda i: (i, 1)),
            pl.BlockSpec((1, d), lambda i: (0, 0)),
            pl.BlockSpec((n_groups, SGU_CHUNK, SGU_CHUNK), lambda i: (0, 0, 0)),
            pl.BlockSpec((SGU_CHUNK, n_groups), lambda i: (0, 0)),
        ],
        out_specs=pl.BlockSpec((tr, d), lambda i: (i, 0)),
        out_shape=jax.ShapeDtypeStruct((s, d), BF16),
        compiler_params=_params("parallel"),
        name="sgu_gate",
    )(z, z, g_v.reshape(1, d), w_s, b_s.T)


def _sb_kernel(q_ref, k_ref, v_ref, o_ref, acc_ref, *, tb, heads, scale_log2):
    i = pl.program_id(1)
    row = lax.broadcasted_iota(jnp.int32, (tb, tb), 0)
    col = lax.broadcasted_iota(jnp.int32, (tb, tb), 1)
    neg_later = jnp.where(row > col, -1.0, 0.0).astype(BF16)
    causal = col < row
    dims = (((1,), (1,)), ((), ()))

    def block(j, carries, diagonal):
        start = pl.multiple_of(j * tb, tb)
        chains = range(heads)
        lanes = [slice(c * HEAD_DIM, (c + 1) * HEAD_DIM) for c in chains]
        z2 = [lax.dot_general(q_ref[:, lanes[c]], k_ref[pl.ds(start, tb), lanes[c]], dims,
                              preferred_element_type=F32) * scale_log2 for c in chains]
        log2_beta, parts, new_carries = [], [], []
        for c in chains:
            sp2 = jnp.maximum(z2[c], 0.0) + jnp.log(1.0 + jnp.exp2(-jnp.abs(z2[c]))) * LOG2_E
            log2_beta.append(z2[c] - sp2)
            if diagonal:
                sp2 = jnp.where(causal, sp2, 0.0)
            hi = pltpu.bitcast(pltpu.bitcast(sp2, jnp.uint32) & jnp.uint32(0xFFFF0000), F32)
            parts.append(jnp.concatenate([hi.astype(BF16), (sp2 - hi).astype(BF16)], axis=0))
            new_carries.append(carries[c] - jnp.sum(sp2, axis=-1, keepdims=True))
        both = [jnp.dot(parts[c], neg_later, preferred_element_type=F32) for c in chains]
        a = []
        for c in chains:
            log2_tail = both[c][:tb] + both[c][tb:] + carries[c]
            a_c = jnp.exp2(log2_beta[c] + log2_tail)
            if diagonal:
                a_c = jnp.where(causal, a_c, 0.0)
            a.append(a_c.astype(BF16))
        for c in chains:
            av = jnp.dot(a[c], v_ref[pl.ds(start, tb), lanes[c]], preferred_element_type=F32)
            if diagonal:
                acc_ref[c] = av
            else:
                acc_ref[c] += av
        return tuple(new_carries)

    def any_live(carries):
        return functools.reduce(jnp.maximum, [jnp.max(c) for c in carries]) > EXP2_TO_ZERO

    def body(state):
        it, carries, _ = state
        carries = block(i - 1 - it, carries, False)
        return it + 1, carries, any_live(carries)

    carries = block(i, tuple(jnp.zeros((tb, 1), F32) for _ in range(heads)), True)
    lax.while_loop(lambda state: jnp.logical_and(state[0] < i, state[2]), body,
                   (jnp.int32(0), carries, any_live(carries)))
    for c in range(heads):
        o_ref[:, c * HEAD_DIM : (c + 1) * HEAD_DIM] = acc_ref[c].astype(o_ref.dtype)


def stick_breaking(qkv, n_heads):
    s = qkv.shape[0]
    tb = _tile(s, 256)
    heads = _tile(n_heads, 4)
    groups = n_heads // heads
    width = heads * HEAD_DIM
    return pl.pallas_call(
        functools.partial(_sb_kernel, tb=tb, heads=heads, scale_log2=LOG2_E / math.sqrt(HEAD_DIM)),
        grid=(groups, s // tb),
        in_specs=[
            pl.BlockSpec((tb, width), lambda h, i: (i, h)),
            pl.BlockSpec((s, width), lambda h, i: (0, groups + h)),
            pl.BlockSpec((s, width), lambda h, i: (0, 2 * groups + h)),
        ],
        out_specs=pl.BlockSpec((tb, width), lambda h, i: (i, h)),
        out_shape=jax.ShapeDtypeStruct((s, n_heads * HEAD_DIM), BF16),
        scratch_shapes=[pltpu.VMEM((heads, tb, HEAD_DIM), F32)],
        compiler_params=_params("parallel", "arbitrary"),
        name="stick_breaking",
    )(qkv, qkv, qkv)


def _dilated_kernel(q_ref, kp_ref, kc_ref, vp_ref, vc_ref, o_ref, lse_ref, *, n_heads, band, scale):
    has_prev = pl.program_id(1) > 0
    qi = lax.broadcasted_iota(jnp.int32, (band, band), 0)
    kj = lax.broadcasted_iota(jnp.int32, (band, band), 1)
    valid_prev = (kj >= qi) & has_prev
    valid_cur = kj <= qi
    lane = lax.broadcasted_iota(jnp.int32, (band, LANES), 1)
    dims = (((1,), (1,)), ((), ()))
    lse_all = jnp.zeros((band, LANES), F32)
    chunk = _tile(n_heads, 4)
    for h0 in range(0, n_heads, chunk):
        heads = range(h0, h0 + chunk)
        lanes = {h: slice(h * HEAD_DIM, (h + 1) * HEAD_DIM) for h in heads}
        zp = {h: lax.dot_general(q_ref[:, lanes[h]], kp_ref[:, lanes[h]], dims, preferred_element_type=F32)
              for h in heads}
        zc = {h: lax.dot_general(q_ref[:, lanes[h]], kc_ref[:, lanes[h]], dims, preferred_element_type=F32)
              for h in heads}
        wp, wc = {}, {}
        for h in heads:
            zp_h = jnp.where(valid_prev, zp[h] * scale, -jnp.inf)
            zc_h = jnp.where(valid_cur, zc[h] * scale, -jnp.inf)
            m = jnp.maximum(jnp.max(zp_h, axis=-1, keepdims=True), jnp.max(zc_h, axis=-1, keepdims=True))
            pp = jnp.exp(zp_h - m)
            pc = jnp.exp(zc_h - m)
            den = jnp.sum(pp, axis=-1, keepdims=True) + jnp.sum(pc, axis=-1, keepdims=True)
            inv = 1.0 / den
            wp[h] = (pp * inv).astype(BF16)
            wc[h] = (pc * inv).astype(BF16)
            lse_all = jnp.where(lane == h, m + jnp.log(den), lse_all)
        for h in heads:
            o_ref[:, lanes[h]] = (jnp.dot(wp[h], vp_ref[:, lanes[h]], preferred_element_type=F32)
                                  + jnp.dot(wc[h], vc_ref[:, lanes[h]], preferred_element_type=F32))
    lse_ref[...] = lse_all


def dilated_group(qkv, n_heads, band):
    dilation, n, _ = qkv.shape
    assert n_heads <= LANES
    width = n_heads * HEAD_DIM
    blk = (None, band, width)
    cur = lambda which: pl.BlockSpec(blk, lambda r, b: (r, b, which))
    prev = lambda which: pl.BlockSpec(blk, lambda r, b: (r, jnp.maximum(b - 1, 0), which))
    return pl.pallas_call(
        functools.partial(_dilated_kernel, n_heads=n_heads, band=band, scale=1.0 / math.sqrt(HEAD_DIM)),
        grid=(dilation, n // band),
        in_specs=[cur(0), prev(1), cur(1), prev(2), cur(2)],
        out_specs=[
            pl.BlockSpec(blk, lambda r, b: (r, b, 0)),
            pl.BlockSpec((None, band, LANES), lambda r, b: (r, b, 0)),
        ],
        out_shape=[
            jax.ShapeDtypeStruct((dilation, n, width), F32),
            jax.ShapeDtypeStruct((dilation, n, LANES), F32),
        ],
        compiler_params=_params("parallel", "arbitrary"),
        name=f"dilated_d{dilation}",
    )(qkv, qkv, qkv, qkv, qkv)


def _mix_kernel(*refs, dilations, n_heads):
    n_groups = len(dilations)
    o_refs, lse_refs = refs[:n_groups], refs[n_groups : 2 * n_groups]
    out_ref, o_seq, lse_seq = refs[2 * n_groups :]
    tr = out_ref.shape[0]
    for g, d in enumerate(dilations):
        for r in range(d):
            rows = pl.ds(r, tr // d, stride=d)
            lse_seq[g, rows, :] = lse_refs[g][r]
            for h in range(n_heads):
                o_seq[g, h, rows, :] = o_refs[g][r, :, h * HEAD_DIM : (h + 1) * HEAD_DIM]
    lses = [lse_seq[g] for g in range(n_groups)]
    m = functools.reduce(jnp.maximum, lses)
    es = [jnp.exp(l - m) for l in lses]
    total = functools.reduce(jnp.add, es)
    alphas = [e / total for e in es]
    for h in range(n_heads):
        acc = alphas[0][:, h : h + 1] * o_seq[0, h]
        for g in range(1, n_groups):
            acc = acc + alphas[g][:, h : h + 1] * o_seq[g, h]
        out_ref[:, h * HEAD_DIM : (h + 1) * HEAD_DIM] = acc.astype(out_ref.dtype)


def mix_groups(outs, lses):
    n_groups = len(outs)
    dilations = tuple(o.shape[0] for o in outs)
    width = outs[0].shape[2]
    n_heads = width // HEAD_DIM
    s = outs[0].shape[0] * outs[0].shape[1]
    tr = _tile(s, 256)
    return pl.pallas_call(
        functools.partial(_mix_kernel, dilations=dilations, n_heads=n_heads),
        grid=(s // tr,),
        in_specs=[pl.BlockSpec((d, tr // d, width), lambda i: (0, i, 0)) for d in dilations]
        + [pl.BlockSpec((d, tr // d, LANES), lambda i: (0, i, 0)) for d in dilations],
        out_specs=pl.BlockSpec((tr, width), lambda i: (i, 0)),
        out_shape=jax.ShapeDtypeStruct((s, width), BF16),
        scratch_shapes=[pltpu.VMEM((n_groups, n_heads, tr, HEAD_DIM), F32), pltpu.VMEM((n_groups, tr, LANES), F32)],
        compiler_params=_params("parallel"),
        name="mix_groups",
    )(*outs, *lses)


def kernel(x, norm_mix, norm_ffn, norm_final, a_w_in, a_g_v, a_w_s, a_b_s, a_w_out,
           b_w_qkv, b_w_o, c_w_qkv, c_w_o, w_gate, w_up, w_down):
    batch, seq, d_model = x.shape
    depth = norm_mix.shape[0]
    n_dil = len(DIL_PAIRS)
    a_w_in, a_w_out, b_w_qkv, b_w_o, c_w_qkv, c_w_o, w_gate, w_up, w_down = (
        w.astype(BF16) for w in (a_w_in, a_w_out, b_w_qkv, b_w_o, c_w_qkv, c_w_o, w_gate, w_up, w_down))
    outs = []
    for b in range(batch):
        h = x[b]
        for i in range(depth):
            kind, j = i % N_MIXERS, i // N_MIXERS
            hn = rmsnorm(h, norm_mix[i], BF16)
            if kind == 0:
                z = matmul(hn, a_w_in, j, F32, gelu=True)
                gated = sgu_gate(z, a_g_v[j], a_w_s[j], a_b_s[j])
                h = matmul_residual(gated, a_w_out, j, h)
            elif kind == 1:
                n_heads = b_w_o.shape[1] // HEAD_DIM
                qkv = matmul(hn, b_w_qkv, j, BF16)
                o = stick_breaking(qkv, n_heads)
                h = matmul_residual(o, b_w_o, j, h)
            else:
                n_heads = c_w_o.shape[1] // HEAD_DIM
                parts = []
                for g, (window, dilation) in enumerate(DIL_PAIRS):
                    qkv = matmul_by_residue(hn, c_w_qkv, j, g, n_dil, dilation)
                    parts.append(dilated_group(qkv, n_heads, window // dilation))
                o = mix_groups([p[0] for p in parts], [p[1] for p in parts])
                h = matmul_residual(o, c_w_o, j, h)
            hf = rmsnorm(h, norm_ffn[i], BF16)
            act = gate_up(hf, w_gate, w_up, i)
            h = matmul_residual(act, w_down, i, h)
        outs.append(rmsnorm(h, norm_final, F32))
    return jnp.stack(outs, axis=0)
```

```python
import functools
import math

import jax
import jax.numpy as jnp
from jax import lax
from jax.experimental import pallas as pl
from jax.experimental.pallas import tpu as pltpu

EPS = 1e-6
LANES = 128
HEAD_DIM = 128
SGU_CHUNK = 128
DIL_PAIRS = ((128, 1), (512, 4), (2048, 16))
N_MIXERS = 3

LOG2_E = 1.4426950408889634
EXP2_TO_ZERO = -151.0

F32 = jnp.float32
BF16 = jnp.bfloat16

VMEM_LIMIT_BYTES = 56 * 1024 * 1024


def _params(*semantics):
    return pltpu.CompilerParams(dimension_semantics=semantics, vmem_limit_bytes=VMEM_LIMIT_BYTES)


def _tile(dim, preferred):
    t = min(dim, preferred)
    while dim % t:
        t //= 2
    return t


def _rmsnorm_kernel(x_ref, g_ref, o_ref):
    x = x_ref[...]
    ms = jnp.mean(x * x, axis=-1, keepdims=True)
    o_ref[...] = (x * lax.rsqrt(ms + EPS) * g_ref[...]).astype(o_ref.dtype)


def rmsnorm(x, g, out_dtype):
    s, d = x.shape
    tr = _tile(s, 256)
    return pl.pallas_call(
        _rmsnorm_kernel,
        grid=(s // tr,),
        in_specs=[pl.BlockSpec((tr, d), lambda i: (i, 0)), pl.BlockSpec((1, d), lambda i: (0, 0))],
        out_specs=pl.BlockSpec((tr, d), lambda i: (i, 0)),
        out_shape=jax.ShapeDtypeStruct((s, d), out_dtype),
        compiler_params=_params("parallel"),
        name="rmsnorm",
    )(x, g.reshape(1, d))


def _gelu_exact(x):
    return 0.5 * x * (1.0 + lax.erf(x * (1.0 / math.sqrt(2.0))))


def _mm_kernel(x_ref, w_ref, o_ref, *, gelu):
    acc = jnp.dot(x_ref[...], w_ref[...].astype(BF16), preferred_element_type=F32)
    if gelu:
        acc = _gelu_exact(acc)
    o_ref[...] = acc.astype(o_ref.dtype)


def _weight_spec(k, tn, layer):
    return pl.BlockSpec((None, k, tn), lambda i, j: (layer, 0, j))


def _stationary_weight_spec(k, tn, layer, col_block0=0):
    return pl.BlockSpec((None, k, tn), lambda j, i: (layer, 0, col_block0 + j))


def _row_spec(tm, k):
    return pl.BlockSpec((tm, k), lambda j, i: (i, 0))


def matmul(x, w, layer, out_dtype, *, gelu=False, tm=1024, tn=512):
    m, k = x.shape
    n = w.shape[2]
    tm, tn = _tile(m, tm), _tile(n, tn)
    return pl.pallas_call(
        functools.partial(_mm_kernel, gelu=gelu),
        grid=(n // tn, m // tm),
        in_specs=[_row_spec(tm, k), _stationary_weight_spec(k, tn, layer)],
        out_specs=pl.BlockSpec((tm, tn), lambda j, i: (i, j)),
        out_shape=jax.ShapeDtypeStruct((m, n), out_dtype),
        compiler_params=_params("parallel", "arbitrary"),
        name="matmul_gelu" if gelu else "matmul",
    )(x, w)


def _mm_by_residue_kernel(x_ref, w_ref, o_ref, acc_ref, *, dilation):
    acc = jnp.dot(x_ref[...], w_ref[...].astype(BF16), preferred_element_type=F32)
    rows = acc.shape[0] // dilation
    for c in range(acc_ref.shape[0]):
        lanes = slice(c * LANES, (c + 1) * LANES)
        acc_ref[c] = acc[:, lanes]
        for r in range(dilation):
            o_ref[r, :, lanes] = acc_ref[c, pl.ds(r, rows, stride=dilation), :].astype(o_ref.dtype)


def matmul_by_residue(x, w, layer, group, n_groups, dilation, *, tm=1024, tn=512):
    m, k = x.shape
    n = w.shape[2] // n_groups
    tm, tn = _tile(m, tm), _tile(n, tn)
    return pl.pallas_call(
        functools.partial(_mm_by_residue_kernel, dilation=dilation),
        grid=(n // tn, m // tm),
        in_specs=[_row_spec(tm, k), _stationary_weight_spec(k, tn, layer, group * (n // tn))],
        out_specs=pl.BlockSpec((dilation, tm // dilation, tn), lambda j, i: (0, i, j)),
        out_shape=jax.ShapeDtypeStruct((dilation, m // dilation, n), BF16),
        scratch_shapes=[pltpu.VMEM((tn // LANES, tm, LANES), F32)],
        compiler_params=_params("parallel", "arbitrary"),
        name=f"matmul_by_residue_d{dilation}",
    )(x, w)


def _mm_residual_kernel(x_ref, w_ref, r_ref, o_ref):
    acc = jnp.dot(x_ref[...], w_ref[...], preferred_element_type=F32)
    o_ref[...] = r_ref[...] + acc


def matmul_residual(x, w, layer, res):
    m, k = x.shape
    n = w.shape[2]
    tm, tn = (_tile(m, 1024), _tile(n, 1024)) if k <= 4096 else (_tile(m, 512), _tile(n, 512))
    return pl.pallas_call(
        _mm_residual_kernel,
        grid=(m // tm, n // tn),
        in_specs=[
            pl.BlockSpec((tm, k), lambda i, j: (i, 0)),
            _weight_spec(k, tn, layer),
            pl.BlockSpec((tm, tn), lambda i, j: (i, j)),
        ],
        out_specs=pl.BlockSpec((tm, tn), lambda i, j: (i, j)),
        out_shape=jax.ShapeDtypeStruct((m, n), F32),
        compiler_params=_params("parallel", "arbitrary"),
        name="matmul_residual",
    )(x, w, res)


def _gate_up_kernel(x_ref, wg_ref, wu_ref, o_ref):
    x = x_ref[...]
    g = jnp.dot(x, wg_ref[...].astype(BF16), preferred_element_type=F32)
    u = jnp.dot(x, wu_ref[...].astype(BF16), preferred_element_type=F32)
    o_ref[...] = (g * jax.nn.sigmoid(g) * u).astype(o_ref.dtype)


def gate_up(x, wg, wu, layer, *, tm=1024, tn=256):
    m, k = x.shape
    n = wg.shape[2]
    tm, tn = _tile(m, tm), _tile(n, tn)
    return pl.pallas_call(
        _gate_up_kernel,
        grid=(n // tn, m // tm),
        in_specs=[_row_spec(tm, k), _stationary_weight_spec(k, tn, layer), _stationary_weight_spec(k, tn, layer)],
        out_specs=pl.BlockSpec((tm, tn), lambda j, i: (i, j)),
        out_shape=jax.ShapeDtypeStruct((m, n), BF16),
        compiler_params=_params("parallel", "arbitrary"),
        name="gate_up",
    )(x, wg, wu)


def _sgu_kernel(u_ref, v_ref, gv_ref, ws_ref, bs_ref, o_ref, *, n_groups, chunks):
    v = v_ref[...]
    ms = jnp.mean(v * v, axis=-1, keepdims=True)
    vn = (v * lax.rsqrt(ms + EPS) * gv_ref[...]).astype(BF16)
    row = lax.broadcasted_iota(jnp.int32, (SGU_CHUNK, SGU_CHUNK), 0)
    col = lax.broadcasted_iota(jnp.int32, (SGU_CHUNK, SGU_CHUNK), 1)
    bs = bs_ref[...]
    for g in range(n_groups):
        w = jnp.where(col <= row, ws_ref[g], 0.0).astype(BF16)
        b = bs[:, g : g + 1]
        lanes = slice(g * HEAD_DIM, (g + 1) * HEAD_DIM)
        for c in range(chunks):
            rows = slice(c * SGU_CHUNK, (c + 1) * SGU_CHUNK)
            mixed = jnp.dot(w, vn[rows, lanes], preferred_element_type=F32) + b
            o_ref[rows, lanes] = (u_ref[rows, lanes] * mixed).astype(o_ref.dtype)


def sgu_gate(z, g_v, w_s, b_s):
    s, two_d = z.shape
    d = two_d // 2
    n_groups = d // HEAD_DIM
    chunks = 2 if s % (2 * SGU_CHUNK) == 0 else 1
    tr = chunks * SGU_CHUNK
    return pl.pallas_call(
        functools.partial(_sgu_kernel, n_groups=n_groups, chunks=chunks),
        grid=(s // tr,),
        in_specs=[
            pl.BlockSpec((tr, d), lambda i: (i, 0)),
            pl.BlockSpec((tr, d), lambda i: (i, 1)),
            pl.BlockSpec((1, d), lambda i: (0, 0)),
            pl.BlockSpec((n_groups, SGU_CHUNK, SGU_CHUNK), lambda i: (0, 0, 0)),
            pl.BlockSpec((SGU_CHUNK, n_groups), lambda i: (0, 0)),
        ],
        out_specs=pl.BlockSpec((tr, d), lambda i: (i, 0)),
        out_shape=jax.ShapeDtypeStruct((s, d), BF16),
        compiler_params=_params("parallel"),
        name="sgu_gate",
    )(z, z, g_v.reshape(1, d), w_s, b_s.T)


def _sb_kernel(q_ref, k_ref, v_ref, o_ref, acc_ref, *, tb, heads, scale_log2):
    i = pl.program_id(1)
    row = lax.broadcasted_iota(jnp.int32, (tb, tb), 0)
    col = lax.broadcasted_iota(jnp.int32, (tb, tb), 1)
    neg_later = jnp.where(row > col, -1.0, 0.0).astype(BF16)
    causal = col < row
    dims = (((1,), (1,)), ((), ()))

    def block(j, carries, diagonal):
        start = pl.multiple_of(j * tb, tb)
        chains = range(heads)
        lanes = [slice(c * HEAD_DIM, (c + 1) * HEAD_DIM) for c in chains]
        z2 = [lax.dot_general(q_ref[:, lanes[c]], k_ref[pl.ds(start, tb), lanes[c]], dims,
                              preferred_element_type=F32) * scale_log2 for c in chains]
        log2_beta, parts, new_carries = [], [], []
        for c in chains:
            sp2 = jnp.maximum(z2[c], 0.0) + jnp.log(1.0 + jnp.exp2(-jnp.abs(z2[c]))) * LOG2_E
            log2_beta.append(z2[c] - sp2)
            if diagonal:
                sp2 = jnp.where(causal, sp2, 0.0)
            hi = pltpu.bitcast(pltpu.bitcast(sp2, jnp.uint32) & jnp.uint32(0xFFFF0000), F32)
            parts.append(jnp.concatenate([hi.astype(BF16), (sp2 - hi).astype(BF16)], axis=0))
            new_carries.append(carries[c] - jnp.sum(sp2, axis=-1, keepdims=True))
        both = [jnp.dot(parts[c], neg_later, preferred_element_type=F32) for c in chains]
        a = []
        for c in chains:
            log2_tail = both[c][:tb] + both[c][tb:] + carries[c]
            a_c = jnp.exp2(log2_beta[c] + log2_tail)
            if diagonal:
                a_c = jnp.where(causal, a_c, 0.0)
            a.append(a_c.astype(BF16))
        for c in chains:
            av = jnp.dot(a[c], v_ref[pl.ds(start, tb), lanes[c]], preferred_element_type=F32)
            if diagonal:
                acc_ref[c] = av
            else:
                acc_ref[c] += av
        return tuple(new_carries)

    def any_live(carries):
        return functools.reduce(jnp.maximum, [jnp.max(c) for c in carries]) > EXP2_TO_ZERO

    def body(state):
        it, carries, _ = state
        carries = block(i - 1 - it, carries, False)
        return it + 1, carries, any_live(carries)

    carries = block(i, tuple(jnp.zeros((tb, 1), F32) for _ in range(heads)), True)
    lax.while_loop(lambda state: jnp.logical_and(state[0] < i, state[2]), body,
                   (jnp.int32(0), carries, any_live(carries)))
    for c in range(heads):
        o_ref[:, c * HEAD_DIM : (c + 1) * HEAD_DIM] = acc_ref[c].astype(o_ref.dtype)


def stick_breaking(qkv, n_heads):
    s = qkv.shape[0]
    tb = _tile(s, 256)
    heads = _tile(n_heads, 4)
    groups = n_heads // heads
    width = heads * HEAD_DIM
    return pl.pallas_call(
        functools.partial(_sb_kernel, tb=tb, heads=heads, scale_log2=LOG2_E / math.sqrt(HEAD_DIM)),
        grid=(groups, s // tb),
        in_specs=[
            pl.BlockSpec((tb, width), lambda h, i: (i, h)),
            pl.BlockSpec((s, width), lambda h, i: (0, groups + h)),
            pl.BlockSpec((s, width), lambda h, i: (0, 2 * groups + h)),
        ],
        out_specs=pl.BlockSpec((tb, width), lambda h, i: (i, h)),
        out_shape=jax.ShapeDtypeStruct((s, n_heads * HEAD_DIM), BF16),
        scratch_shapes=[pltpu.VMEM((heads, tb, HEAD_DIM), F32)],
        compiler_params=_params("parallel", "arbitrary"),
        name="stick_breaking",
    )(qkv, qkv, qkv)


def _dilated_kernel(q_ref, kp_ref, kc_ref, vp_ref, vc_ref, o_ref, lse_ref, *, n_heads, band, scale):
    has_prev = pl.program_id(1) > 0
    qi = lax.broadcasted_iota(jnp.int32, (band, band), 0)
    kj = lax.broadcasted_iota(jnp.int32, (band, band), 1)
    valid_prev = (kj >= qi) & has_prev
    valid_cur = kj <= qi
    lane = lax.broadcasted_iota(jnp.int32, (band, LANES), 1)
    dims = (((1,), (1,)), ((), ()))
    lse_all = jnp.zeros((band, LANES), F32)
    chunk = _tile(n_heads, 4)
    for h0 in range(0, n_heads, chunk):
        heads = range(h0, h0 + chunk)
        lanes = {h: slice(h * HEAD_DIM, (h + 1) * HEAD_DIM) for h in heads}
        zp = {h: lax.dot_general(q_ref[:, lanes[h]], kp_ref[:, lanes[h]], dims, preferred_element_type=F32)
              for h in heads}
        zc = {h: lax.dot_general(q_ref[:, lanes[h]], kc_ref[:, lanes[h]], dims, preferred_element_type=F32)
              for h in heads}
        wp, wc = {}, {}
        for h in heads:
            zp_h = jnp.where(valid_prev, zp[h] * scale, -jnp.inf)
            zc_h = jnp.where(valid_cur, zc[h] * scale, -jnp.inf)
            m = jnp.max(jnp.maximum(zp_h, zc_h), axis=-1, keepdims=True)
            pp = jnp.exp(zp_h - m)
            pc = jnp.exp(zc_h - m)
            den = jnp.sum(pp + pc, axis=-1, keepdims=True)
            inv = 1.0 / den
            wp[h] = (pp * inv).astype(BF16)
            wc[h] = (pc * inv).astype(BF16)
            lse_all = jnp.where(lane == h, m + jnp.log(den), lse_all)
        for h in heads:
            o_ref[:, lanes[h]] = (jnp.dot(wp[h], vp_ref[:, lanes[h]], preferred_element_type=F32)
                                  + jnp.dot(wc[h], vc_ref[:, lanes[h]], preferred_element_type=F32))
    lse_ref[...] = lse_all


def dilated_group(qkv, n_heads, band):
    dilation, n, _ = qkv.shape
    assert n_heads <= LANES
    width = n_heads * HEAD_DIM
    blk = (None, band, width)
    cur = lambda which: pl.BlockSpec(blk, lambda r, b: (r, b, which))
    prev = lambda which: pl.BlockSpec(blk, lambda r, b: (r, jnp.maximum(b - 1, 0), which))
    return pl.pallas_call(
        functools.partial(_dilated_kernel, n_heads=n_heads, band=band, scale=1.0 / math.sqrt(HEAD_DIM)),
        grid=(dilation, n // band),
        in_specs=[cur(0), prev(1), cur(1), prev(2), cur(2)],
        out_specs=[
            pl.BlockSpec(blk, lambda r, b: (r, b, 0)),
            pl.BlockSpec((None, band, LANES), lambda r, b: (r, b, 0)),
        ],
        out_shape=[
            jax.ShapeDtypeStruct((dilation, n, width), F32),
            jax.ShapeDtypeStruct((dilation, n, LANES), F32),
        ],
        compiler_params=_params("parallel", "arbitrary"),
        name=f"dilated_d{dilation}",
    )(qkv, qkv, qkv, qkv, qkv)


def _mix_kernel(*refs, dilations, n_heads):
    n_groups = len(dilations)
    o_refs, lse_refs = refs[:n_groups], refs[n_groups : 2 * n_groups]
    out_ref, o_seq, lse_seq = refs[2 * n_groups :]
    tr = out_ref.shape[0]
    for g, d in enumerate(dilations):
        for r in range(d):
            rows = pl.ds(r, tr // d, stride=d)
            lse_seq[g, rows, :] = lse_refs[g][r]
            for h in range(n_heads):
                o_seq[g, h, rows, :] = o_refs[g][r, :, h * HEAD_DIM : (h + 1) * HEAD_DIM]
    lses = [lse_seq[g] for g in range(n_groups)]
    m = functools.reduce(jnp.maximum, lses)
    es = [jnp.exp(l - m) for l in lses]
    total = functools.reduce(jnp.add, es)
    alphas = [e / total for e in es]
    for h in range(n_heads):
        acc = alphas[0][:, h : h + 1] * o_seq[0, h]
        for g in range(1, n_groups):
            acc = acc + alphas[g][:, h : h + 1] * o_seq[g, h]
        out_ref[:, h * HEAD_DIM : (h + 1) * HEAD_DIM] = acc.astype(out_ref.dtype)


def mix_groups(outs, lses):
    n_groups = len(outs)
    dilations = tuple(o.shape[0] for o in outs)
    width = outs[0].shape[2]
    n_heads = width // HEAD_DIM
    s = outs[0].shape[0] * outs[0].shape[1]
    tr = _tile(s, 256)
    return pl.pallas_call(
        functools.partial(_mix_kernel, dilations=dilations, n_heads=n_heads),
        grid=(s // tr,),
        in_specs=[pl.BlockSpec((d, tr // d, width), lambda i: (0, i, 0)) for d in dilations]
        + [pl.BlockSpec((d, tr // d, LANES), lambda i: (0, i, 0)) for d in dilations],
        out_specs=pl.BlockSpec((tr, width), lambda i: (i, 0)),
        out_shape=jax.ShapeDtypeStruct((s, width), BF16),
        scratch_shapes=[pltpu.VMEM((n_groups, n_heads, tr, HEAD_DIM), F32), pltpu.VMEM((n_groups, tr, LANES), F32)],
        compiler_params=_params("parallel"),
        name="mix_groups",
    )(*outs, *lses)


def kernel(x, norm_mix, norm_ffn, norm_final, a_w_in, a_g_v, a_w_s, a_b_s, a_w_out,
           b_w_qkv, b_w_o, c_w_qkv, c_w_o, w_gate, w_up, w_down):
    batch, seq, d_model = x.shape
    depth = norm_mix.shape[0]
    n_dil = len(DIL_PAIRS)
    a_w_out, b_w_o, c_w_o, w_down = (w.astype(BF16) for w in (a_w_out, b_w_o, c_w_o, w_down))
    outs = []
    for b in range(batch):
        h = x[b]
        for i in range(depth):
            kind, j = i % N_MIXERS, i // N_MIXERS
            hn = rmsnorm(h, norm_mix[i], BF16)
            if kind == 0:
                z = matmul(hn, a_w_in, j, F32, gelu=True)
                gated = sgu_gate(z, a_g_v[j], a_w_s[j], a_b_s[j])
                h = matmul_residual(gated, a_w_out, j, h)
            elif kind == 1:
                n_heads = b_w_o.shape[1] // HEAD_DIM
                qkv = matmul(hn, b_w_qkv, j, BF16)
                o = stick_breaking(qkv, n_heads)
                h = matmul_residual(o, b_w_o, j, h)
            else:
                n_heads = c_w_o.shape[1] // HEAD_DIM
                parts = []
                for g, (window, dilation) in enumerate(DIL_PAIRS):
                    qkv = matmul_by_residue(hn, c_w_qkv, j, g, n_dil, dilation)
                    parts.append(dilated_group(qkv, n_heads, window // dilation))
                o = mix_groups([p[0] for p in parts], [p[1] for p in parts])
                h = matmul_residual(o, c_w_o, j, h)
            hf = rmsnorm(h, norm_ffn[i], BF16)
            act = gate_up(hf, w_gate, w_up, i)
            h = matmul_residual(act, w_down, i, h)
        outs.append(rmsnorm(h, norm_final, F32))
    return jnp.stack(outs, axis=0)
```

```python
import functools
import math

import jax
import jax.numpy as jnp
from jax import lax
from jax.experimental import pallas as pl
from jax.experimental.pallas import tpu as pltpu

EPS = 1e-6
LANES = 128
HEAD_DIM = 128
SGU_CHUNK = 128
DIL_PAIRS = ((128, 1), (512, 4), (2048, 16))
N_MIXERS = 3

LOG2_E = 1.4426950408889634
EXP2_TO_ZERO = -151.0

F32 = jnp.float32
BF16 = jnp.bfloat16

VMEM_LIMIT_BYTES = 56 * 1024 * 1024


def _params(*semantics):
    return pltpu.CompilerParams(dimension_semantics=semantics, vmem_limit_bytes=VMEM_LIMIT_BYTES)


def _tile(dim, preferred):
    t = min(dim, preferred)
    while dim % t:
        t //= 2
    return t


def _rmsnorm_kernel(x_ref, g_ref, o_ref):
    x = x_ref[...]
    ms = jnp.mean(x * x, axis=-1, keepdims=True)
    o_ref[...] = (x * lax.rsqrt(ms + EPS) * g_ref[...]).astype(o_ref.dtype)


def rmsnorm(x, g, out_dtype):
    s, d = x.shape
    tr = _tile(s, 256)
    return pl.pallas_call(
        _rmsnorm_kernel,
        grid=(s // tr,),
        in_specs=[pl.BlockSpec((tr, d), lambda i: (i, 0)), pl.BlockSpec((1, d), lambda i: (0, 0))],
        out_specs=pl.BlockSpec((tr, d), lambda i: (i, 0)),
        out_shape=jax.ShapeDtypeStruct((s, d), out_dtype),
        compiler_params=_params("parallel"),
        name="rmsnorm",
    )(x, g.reshape(1, d))


def _cast_stats_kernel(x_ref, g_ref, b_ref, s_ref):
    x = x_ref[...]
    b_ref[...] = (x * g_ref[...]).astype(BF16)
    s_ref[...] = jnp.sum(x * x, axis=-1, keepdims=True)


def scale_with_row_stats(x, gain):
    s, d = x.shape
    tr = _tile(s, 256)
    return pl.pallas_call(
        _cast_stats_kernel,
        grid=(s // tr,),
        in_specs=[pl.BlockSpec((tr, d), lambda i: (i, 0)), pl.BlockSpec((1, d), lambda i: (0, 0))],
        out_specs=[pl.BlockSpec((tr, d), lambda i: (i, 0)), pl.BlockSpec((tr, 1), lambda i: (i, 0))],
        out_shape=[jax.ShapeDtypeStruct((s, d), BF16), jax.ShapeDtypeStruct((s, 1), F32)],
        compiler_params=_params("parallel"),
        name="scale_with_row_stats",
    )(x, gain.reshape(1, d))


def _gelu_exact(x):
    return 0.5 * x * (1.0 + lax.erf(x * (1.0 / math.sqrt(2.0))))


def _row_rsqrt(ss_ref, k):
    return lax.rsqrt(ss_ref[...] * (1.0 / k) + EPS)


def _prenorm_specs(tm, k):
    return [pl.BlockSpec((tm, k), lambda j, i: (i, 0)), pl.BlockSpec((tm, 1), lambda j, i: (i, 0))]


def _mm_kernel(x_ref, ss_ref, w_ref, o_ref, *, gelu):
    acc = jnp.dot(x_ref[...], w_ref[...].astype(BF16), preferred_element_type=F32)
    acc = acc * _row_rsqrt(ss_ref, x_ref.shape[1])
    if gelu:
        acc = _gelu_exact(acc)
    o_ref[...] = acc.astype(o_ref.dtype)


def _weight_spec(k, tn, layer):
    return pl.BlockSpec((None, k, tn), lambda i, j: (layer, 0, j))


def _stationary_weight_spec(k, tn, layer, col_block0=0):
    return pl.BlockSpec((None, k, tn), lambda j, i: (layer, 0, col_block0 + j))


def matmul(x, ss, w, layer, out_dtype, *, gelu=False, tm=512, tn=1024):
    m, k = x.shape
    n = w.shape[2]
    tm, tn = _tile(m, tm), _tile(n, tn)
    return pl.pallas_call(
        functools.partial(_mm_kernel, gelu=gelu),
        grid=(n // tn, m // tm),
        in_specs=_prenorm_specs(tm, k) + [_stationary_weight_spec(k, tn, layer)],
        out_specs=pl.BlockSpec((tm, tn), lambda j, i: (i, j)),
        out_shape=jax.ShapeDtypeStruct((m, n), out_dtype),
        compiler_params=_params("parallel", "arbitrary"),
        name="matmul_gelu" if gelu else "matmul",
    )(x, ss, w)


def _mm_by_residue_kernel(x_ref, ss_ref, w_ref, o_ref, acc_ref, *, dilation):
    acc = jnp.dot(x_ref[...], w_ref[...].astype(BF16), preferred_element_type=F32)
    acc = acc * _row_rsqrt(ss_ref, x_ref.shape[1])
    rows = acc.shape[0] // dilation
    for c in range(acc_ref.shape[0]):
        lanes = slice(c * LANES, (c + 1) * LANES)
        acc_ref[c] = acc[:, lanes]
        for r in range(dilation):
            o_ref[r, :, lanes] = acc_ref[c, pl.ds(r, rows, stride=dilation), :].astype(o_ref.dtype)


def matmul_by_residue(x, ss, w, layer, group, n_groups, dilation, *, tm=512, tn=1024):
    m, k = x.shape
    n = w.shape[2] // n_groups
    tm, tn = _tile(m, tm), _tile(n, tn)
    return pl.pallas_call(
        functools.partial(_mm_by_residue_kernel, dilation=dilation),
        grid=(n // tn, m // tm),
        in_specs=_prenorm_specs(tm, k) + [_stationary_weight_spec(k, tn, layer, group * (n // tn))],
        out_specs=pl.BlockSpec((dilation, tm // dilation, tn), lambda j, i: (0, i, j)),
        out_shape=jax.ShapeDtypeStruct((dilation, m // dilation, n), BF16),
        scratch_shapes=[pltpu.VMEM((tn // LANES, tm, LANES), F32)],
        compiler_params=_params("parallel", "arbitrary"),
        name=f"matmul_by_residue_d{dilation}",
    )(x, ss, w)


def _mm_residual_kernel(x_ref, w_ref, r_ref, gn_ref, o_ref, ob_ref, ss_ref):
    h = r_ref[...] + jnp.dot(x_ref[...], w_ref[...], preferred_element_type=F32)
    o_ref[...] = h
    ob_ref[...] = (h * gn_ref[...]).astype(BF16)
    part = jnp.sum(h * h, axis=-1, keepdims=True)

    @pl.when(pl.program_id(1) == 0)
    def _():
        ss_ref[...] = part

    @pl.when(pl.program_id(1) > 0)
    def _():
        ss_ref[...] += part


def matmul_residual(x, w, layer, res, next_gain):
    m, k = x.shape
    n = w.shape[2]
    tm, tn = (_tile(m, 1024), _tile(n, 512)) if k <= 4096 else (_tile(m, 512), _tile(n, 512))
    return pl.pallas_call(
        _mm_residual_kernel,
        grid=(m // tm, n // tn),
        in_specs=[
            pl.BlockSpec((tm, k), lambda i, j: (i, 0)),
            _weight_spec(k, tn, layer),
            pl.BlockSpec((tm, tn), lambda i, j: (i, j)),
            pl.BlockSpec((1, tn), lambda i, j: (0, j)),
        ],
        out_specs=[pl.BlockSpec((tm, tn), lambda i, j: (i, j)), pl.BlockSpec((tm, tn), lambda i, j: (i, j)),
                   pl.BlockSpec((tm, 1), lambda i, j: (i, 0))],
        out_shape=[jax.ShapeDtypeStruct((m, n), F32), jax.ShapeDtypeStruct((m, n), BF16),
                   jax.ShapeDtypeStruct((m, 1), F32)],
        compiler_params=_params("parallel", "arbitrary"),
        name="matmul_residual",
    )(x, w, res, next_gain.reshape(1, n))


def _gate_up_kernel(x_ref, ss_ref, wg_ref, wu_ref, o_ref):
    x = x_ref[...]
    r = _row_rsqrt(ss_ref, x_ref.shape[1])
    g = jnp.dot(x, wg_ref[...].astype(BF16), preferred_element_type=F32) * r
    u = jnp.dot(x, wu_ref[...].astype(BF16), preferred_element_type=F32) * r
    o_ref[...] = (g * jax.nn.sigmoid(g) * u).astype(o_ref.dtype)


def gate_up(x, ss, wg, wu, layer, *, tm=1024, tn=256):
    m, k = x.shape
    n = wg.shape[2]
    tm, tn = _tile(m, tm), _tile(n, tn)
    return pl.pallas_call(
        _gate_up_kernel,
        grid=(n // tn, m // tm),
        in_specs=_prenorm_specs(tm, k) + [_stationary_weight_spec(k, tn, layer)] * 2,
        out_specs=pl.BlockSpec((tm, tn), lambda j, i: (i, j)),
        out_shape=jax.ShapeDtypeStruct((m, n), BF16),
        compiler_params=_params("parallel", "arbitrary"),
        name="gate_up",
    )(x, ss, wg, wu)


def _sgu_kernel(u_ref, v_ref, gv_ref, ws_ref, bs_ref, o_ref, *, n_groups, chunks):
    v = v_ref[...]
    ms = jnp.mean(v * v, axis=-1, keepdims=True)
    vn = (v * lax.rsqrt(ms + EPS) * gv_ref[...]).astype(BF16)
    row = lax.broadcasted_iota(jnp.int32, (SGU_CHUNK, SGU_CHUNK), 0)
    col = lax.broadcasted_iota(jnp.int32, (SGU_CHUNK, SGU_CHUNK), 1)
    bs = bs_ref[...]
    for g in range(n_groups):
        w = jnp.where(col <= row, ws_ref[g], 0.0).astype(BF16)
        b = bs[:, g : g + 1]
        lanes = slice(g * HEAD_DIM, (g + 1) * HEAD_DIM)
        for c in range(chunks):
            rows = slice(c * SGU_CHUNK, (c + 1) * SGU_CHUNK)
            mixed = jnp.dot(w, vn[rows, lanes], preferred_element_type=F32) + b
            o_ref[rows, lanes] = (u_ref[rows, lanes] * mixed).astype(o_ref.dtype)


def sgu_gate(z, g_v, w_s, b_s):
    s, two_d = z.shape
    d = two_d // 2
    n_groups = d // HEAD_DIM
    chunks = 2 if s % (2 * SGU_CHUNK) == 0 else 1
    tr = chunks * SGU_CHUNK
    return pl.pallas_call(
        functools.partial(_sgu_kernel, n_groups=n_groups, chunks=chunks),
        grid=(s // tr,),
        in_specs=[
            pl.BlockSpec((tr, d), lambda i: (i, 0)),
            pl.BlockSpec((tr, d), lambda i: (i, 1)),
            pl.BlockSpec((1, d), lambda i: (0, 0)),
            pl.BlockSpec((n_groups, SGU_CHUNK, SGU_CHUNK), lambda i: (0, 0, 0)),
            pl.BlockSpec((SGU_CHUNK, n_groups), lambda i: (0, 0)),
        ],
        out_specs=pl.BlockSpec((tr, d), lambda i: (i, 0)),
        out_shape=jax.ShapeDtypeStruct((s, d), BF16),
        compiler_params=_params("parallel"),
        name="sgu_gate",
    )(z, z, g_v.reshape(1, d), w_s, b_s.T)


def _sb_kernel(q_ref, k_ref, v_ref, o_ref, acc_ref, *, tb, heads, scale_log2):
    i = pl.program_id(1)
    row = lax.broadcasted_iota(jnp.int32, (tb, tb), 0)
    col = lax.broadcasted_iota(jnp.int32, (tb, tb), 1)
    neg_later = jnp.where(row > col, -1.0, 0.0).astype(BF16)
    causal = col < row
    dims = (((1,), (1,)), ((), ()))

    def block(j, carries, diagonal):
        start = pl.multiple_of(j * tb, tb)
        chains = range(heads)
        lanes = [slice(c * HEAD_DIM, (c + 1) * HEAD_DIM) for c in chains]
        z2 = [lax.dot_general(q_ref[:, lanes[c]], k_ref[pl.ds(start, tb), lanes[c]], dims,
                              preferred_element_type=F32) * scale_log2 for c in chains]
        log2_beta, parts, new_carries = [], [], []
        for c in chains:
            sp2 = jnp.maximum(z2[c], 0.0) + jnp.log(1.0 + jnp.exp2(-jnp.abs(z2[c]))) * LOG2_E
            log2_beta.append(z2[c] - sp2)
            if diagonal:
                sp2 = jnp.where(causal, sp2, 0.0)
            hi = pltpu.bitcast(pltpu.bitcast(sp2, jnp.uint32) & jnp.uint32(0xFFFF0000), F32)
            parts.append(jnp.concatenate([hi.astype(BF16), (sp2 - hi).astype(BF16)], axis=0))
            new_carries.append(carries[c] - jnp.sum(sp2, axis=-1, keepdims=True))
        both = [jnp.dot(parts[c], neg_later, preferred_element_type=F32) for c in chains]
        a = []
        for c in chains:
            log2_tail = both[c][:tb] + both[c][tb:] + carries[c]
            a_c = jnp.exp2(log2_beta[c] + log2_tail)
            if diagonal:
                a_c = jnp.where(causal, a_c, 0.0)
            a.append(a_c.astype(BF16))
        for c in chains:
            av = jnp.dot(a[c], v_ref[pl.ds(start, tb), lanes[c]], preferred_element_type=F32)
            if diagonal:
                acc_ref[c] = av
            else:
                acc_ref[c] += av
        return tuple(new_carries)

    def any_live(carries):
        return functools.reduce(jnp.maximum, [jnp.max(c) for c in carries]) > EXP2_TO_ZERO

    def body(state):
        it, carries, _ = state
        carries = block(i - 1 - it, carries, False)
        return it + 1, carries, any_live(carries)

    carries = block(i, tuple(jnp.zeros((tb, 1), F32) for _ in range(heads)), True)
    lax.while_loop(lambda state: jnp.logical_and(state[0] < i, state[2]), body,
                   (jnp.int32(0), carries, any_live(carries)))
    for c in range(heads):
        o_ref[:, c * HEAD_DIM : (c + 1) * HEAD_DIM] = acc_ref[c].astype(o_ref.dtype)


def stick_breaking(qkv, n_heads):
    s = qkv.shape[0]
    tb = _tile(s, 256)
    heads = _tile(n_heads, 4)
    groups = n_heads // heads
    width = heads * HEAD_DIM
    return pl.pallas_call(
        functools.partial(_sb_kernel, tb=tb, heads=heads, scale_log2=LOG2_E / math.sqrt(HEAD_DIM)),
        grid=(groups, s // tb),
        in_specs=[
            pl.BlockSpec((tb, width), lambda h, i: (i, h)),
            pl.BlockSpec((s, width), lambda h, i: (0, groups + h)),
            pl.BlockSpec((s, width), lambda h, i: (0, 2 * groups + h)),
        ],
        out_specs=pl.BlockSpec((tb, width), lambda h, i: (i, h)),
        out_shape=jax.ShapeDtypeStruct((s, n_heads * HEAD_DIM), BF16),
        scratch_shapes=[pltpu.VMEM((heads, tb, HEAD_DIM), F32)],
        compiler_params=_params("parallel", "arbitrary"),
        name="stick_breaking",
    )(qkv, qkv, qkv)


def _dilated_kernel(q_ref, kp_ref, kc_ref, vp_ref, vc_ref, o_ref, lse_ref, *, n_heads, band, scale):
    has_prev = pl.program_id(1) > 0
    qi = lax.broadcasted_iota(jnp.int32, (band, band), 0)
    kj = lax.broadcasted_iota(jnp.int32, (band, band), 1)
    valid_prev = (kj >= qi) & has_prev
    valid_cur = kj <= qi
    lane = lax.broadcasted_iota(jnp.int32, (band, LANES), 1)
    dims = (((1,), (1,)), ((), ()))
    lse_all = jnp.zeros((band, LANES), F32)
    chunk = _tile(n_heads, 4)
    for h0 in range(0, n_heads, chunk):
        heads = range(h0, h0 + chunk)
        lanes = {h: slice(h * HEAD_DIM, (h + 1) * HEAD_DIM) for h in heads}
        zp = {h: lax.dot_general(q_ref[:, lanes[h]], kp_ref[:, lanes[h]], dims, preferred_element_type=F32)
              for h in heads}
        zc = {h: lax.dot_general(q_ref[:, lanes[h]], kc_ref[:, lanes[h]], dims, preferred_element_type=F32)
              for h in heads}
        wp, wc = {}, {}
        for h in heads:
            zp_h = jnp.where(valid_prev, zp[h] * scale, -jnp.inf)
            zc_h = jnp.where(valid_cur, zc[h] * scale, -jnp.inf)
            m = jnp.max(jnp.maximum(zp_h, zc_h), axis=-1, keepdims=True)
            pp = jnp.exp(zp_h - m)
            pc = jnp.exp(zc_h - m)
            den = jnp.sum(pp + pc, axis=-1, keepdims=True)
            inv = 1.0 / den
            wp[h] = (pp * inv).astype(BF16)
            wc[h] = (pc * inv).astype(BF16)
            lse_all = jnp.where(lane == h, m + jnp.log(den), lse_all)
        for h in heads:
            o_ref[:, lanes[h]] = (jnp.dot(wp[h], vp_ref[:, lanes[h]], preferred_element_type=F32)
                                  + jnp.dot(wc[h], vc_ref[:, lanes[h]], preferred_element_type=F32))
    lse_ref[...] = lse_all


def dilated_group(qkv, n_heads, band):
    dilation, n, _ = qkv.shape
    assert n_heads <= LANES
    width = n_heads * HEAD_DIM
    blk = (None, band, width)
    cur = lambda which: pl.BlockSpec(blk, lambda r, b: (r, b, which))
    prev = lambda which: pl.BlockSpec(blk, lambda r, b: (r, jnp.maximum(b - 1, 0), which))
    return pl.pallas_call(
        functools.partial(_dilated_kernel, n_heads=n_heads, band=band, scale=1.0 / math.sqrt(HEAD_DIM)),
        grid=(dilation, n // band),
        in_specs=[cur(0), prev(1), cur(1), prev(2), cur(2)],
        out_specs=[
            pl.BlockSpec(blk, lambda r, b: (r, b, 0)),
            pl.BlockSpec((None, band, LANES), lambda r, b: (r, b, 0)),
        ],
        out_shape=[
            jax.ShapeDtypeStruct((dilation, n, width), F32),
            jax.ShapeDtypeStruct((dilation, n, LANES), F32),
        ],
        compiler_params=_params("parallel", "arbitrary"),
        name=f"dilated_d{dilation}",
    )(qkv, qkv, qkv, qkv, qkv)


def _mix_kernel(*refs, dilations, n_heads):
    n_groups = len(dilations)
    o_refs, lse_refs = refs[:n_groups], refs[n_groups : 2 * n_groups]
    out_ref, o_seq, lse_seq = refs[2 * n_groups :]
    tr = out_ref.shape[0]
    for g, d in enumerate(dilations):
        for r in range(d):
            rows = pl.ds(r, tr // d, stride=d)
            lse_seq[g, rows, :] = lse_refs[g][r]
            for h in range(n_heads):
                o_seq[g, h, rows, :] = o_refs[g][r, :, h * HEAD_DIM : (h + 1) * HEAD_DIM]
    lses = [lse_seq[g] for g in range(n_groups)]
    m = functools.reduce(jnp.maximum, lses)
    es = [jnp.exp(l - m) for l in lses]
    total = functools.reduce(jnp.add, es)
    alphas = [e / total for e in es]
    for h in range(n_heads):
        acc = alphas[0][:, h : h + 1] * o_seq[0, h]
        for g in range(1, n_groups):
            acc = acc + alphas[g][:, h : h + 1] * o_seq[g, h]
        out_ref[:, h * HEAD_DIM : (h + 1) * HEAD_DIM] = acc.astype(out_ref.dtype)


def mix_groups(outs, lses):
    n_groups = len(outs)
    dilations = tuple(o.shape[0] for o in outs)
    width = outs[0].shape[2]
    n_heads = width // HEAD_DIM
    s = outs[0].shape[0] * outs[0].shape[1]
    tr = _tile(s, 256)
    return pl.pallas_call(
        functools.partial(_mix_kernel, dilations=dilations, n_heads=n_heads),
        grid=(s // tr,),
        in_specs=[pl.BlockSpec((d, tr // d, width), lambda i: (0, i, 0)) for d in dilations]
        + [pl.BlockSpec((d, tr // d, LANES), lambda i: (0, i, 0)) for d in dilations],
        out_specs=pl.BlockSpec((tr, width), lambda i: (i, 0)),
        out_shape=jax.ShapeDtypeStruct((s, width), BF16),
        scratch_shapes=[pltpu.VMEM((n_groups, n_heads, tr, HEAD_DIM), F32), pltpu.VMEM((n_groups, tr, LANES), F32)],
        compiler_params=_params("parallel"),
        name="mix_groups",
    )(*outs, *lses)


def kernel(x, norm_mix, norm_ffn, norm_final, a_w_in, a_g_v, a_w_s, a_b_s, a_w_out,
           b_w_qkv, b_w_o, c_w_qkv, c_w_o, w_gate, w_up, w_down):
    batch, seq, d_model = x.shape
    depth = norm_mix.shape[0]
    n_dil = len(DIL_PAIRS)
    a_w_out, b_w_o, c_w_o, w_down = (w.astype(BF16) for w in (a_w_out, b_w_o, c_w_o, w_down))
    outs = []
    for b in range(batch):
        h = x[b]
        hb, ss = scale_with_row_stats(h, norm_mix[0])
        for i in range(depth):
            kind, j = i % N_MIXERS, i // N_MIXERS
            if kind == 0:
                z = matmul(hb, ss, a_w_in, j, F32, gelu=True)
                gated = sgu_gate(z, a_g_v[j], a_w_s[j], a_b_s[j])
                h, hb, ss = matmul_residual(gated, a_w_out, j, h, norm_ffn[i])
            elif kind == 1:
                n_heads = b_w_o.shape[1] // HEAD_DIM
                qkv = matmul(hb, ss, b_w_qkv, j, BF16)
                o = stick_breaking(qkv, n_heads)
                h, hb, ss = matmul_residual(o, b_w_o, j, h, norm_ffn[i])
            else:
                n_heads = c_w_o.shape[1] // HEAD_DIM
                parts = []
                for g, (window, dilation) in enumerate(DIL_PAIRS):
                    qkv = matmul_by_residue(hb, ss, c_w_qkv, j, g, n_dil, dilation)
                    parts.append(dilated_group(qkv, n_heads, window // dilation))
                o = mix_groups([p[0] for p in parts], [p[1] for p in parts])
                h, hb, ss = matmul_residual(o, c_w_o, j, h, norm_ffn[i])
            act = gate_up(hb, ss, w_gate, w_up, i)
            next_gain = norm_mix[i + 1] if i + 1 < depth else norm_final
            h, hb, ss = matmul_residual(act, w_down, i, h, next_gain)
        outs.append(rmsnorm(h, norm_final, F32))
    return jnp.stack(outs, axis=0)
```

```python
import functools
import math

import jax
import jax.numpy as jnp
from jax import lax
from jax.experimental import pallas as pl
from jax.experimental.pallas import tpu as pltpu

EPS = 1e-6
LANES = 128
HEAD_DIM = 128
SGU_CHUNK = 128
DIL_PAIRS = ((128, 1), (512, 4), (2048, 16))
N_MIXERS = 3

LOG2_E = 1.4426950408889634
EXP2_TO_ZERO = -151.0

F32 = jnp.float32
BF16 = jnp.bfloat16

VMEM_LIMIT_BYTES = 56 * 1024 * 1024


def _params(*semantics):
    return pltpu.CompilerParams(dimension_semantics=semantics, vmem_limit_bytes=VMEM_LIMIT_BYTES)


def _tile(dim, preferred):
    t = min(dim, preferred)
    while dim % t:
        t //= 2
    return t


def _rmsnorm_kernel(x_ref, g_ref, o_ref):
    x = x_ref[...]
    ms = jnp.mean(x * x, axis=-1, keepdims=True)
    o_ref[...] = (x * lax.rsqrt(ms + EPS) * g_ref[...]).astype(o_ref.dtype)


def rmsnorm(x, g, out_dtype):
    s, d = x.shape
    tr = _tile(s, 256)
    return pl.pallas_call(
        _rmsnorm_kernel,
        grid=(s // tr,),
        in_specs=[pl.BlockSpec((tr, d), lambda i: (i, 0)), pl.BlockSpec((1, d), lambda i: (0, 0))],
        out_specs=pl.BlockSpec((tr, d), lambda i: (i, 0)),
        out_shape=jax.ShapeDtypeStruct((s, d), out_dtype),
        compiler_params=_params("parallel"),
        name="rmsnorm",
    )(x, g.reshape(1, d))


def _cast_stats_kernel(x_ref, g_ref, b_ref, s_ref):
    x = x_ref[...]
    b_ref[...] = (x * g_ref[...]).astype(BF16)
    s_ref[...] = jnp.sum(x * x, axis=-1, keepdims=True)


def scale_with_row_stats(x, gain):
    s, d = x.shape
    tr = _tile(s, 256)
    return pl.pallas_call(
        _cast_stats_kernel,
        grid=(s // tr,),
        in_specs=[pl.BlockSpec((tr, d), lambda i: (i, 0)), pl.BlockSpec((1, d), lambda i: (0, 0))],
        out_specs=[pl.BlockSpec((tr, d), lambda i: (i, 0)), pl.BlockSpec((tr, 1), lambda i: (i, 0))],
        out_shape=[jax.ShapeDtypeStruct((s, d), BF16), jax.ShapeDtypeStruct((s, 1), F32)],
        compiler_params=_params("parallel"),
        name="scale_with_row_stats",
    )(x, gain.reshape(1, d))


def _gelu_exact(x):
    return 0.5 * x * (1.0 + lax.erf(x * (1.0 / math.sqrt(2.0))))


def _row_rsqrt(ss_ref, k):
    return lax.rsqrt(ss_ref[...] * (1.0 / k) + EPS)


def _prenorm_specs(tm, k):
    return [pl.BlockSpec((tm, k), lambda j, i: (i, 0)), pl.BlockSpec((tm, 1), lambda j, i: (i, 0))]


def _mm_kernel(x_ref, ss_ref, w_ref, o_ref, *, gelu):
    acc = jnp.dot(x_ref[...], w_ref[...].astype(BF16), preferred_element_type=F32)
    acc = acc * _row_rsqrt(ss_ref, x_ref.shape[1])
    if gelu:
        acc = _gelu_exact(acc)
    o_ref[...] = acc.astype(o_ref.dtype)


def _weight_spec(k, tn, layer):
    return pl.BlockSpec((None, k, tn), lambda i, j: (layer, 0, j))


def _stationary_weight_spec(k, tn, layer, col_block0=0):
    return pl.BlockSpec((None, k, tn), lambda j, i: (layer, 0, col_block0 + j))


def matmul(x, ss, w, layer, out_dtype, *, gelu=False, tm=512, tn=1024):
    m, k = x.shape
    n = w.shape[2]
    tm, tn = _tile(m, tm), _tile(n, tn)
    return pl.pallas_call(
        functools.partial(_mm_kernel, gelu=gelu),
        grid=(n // tn, m // tm),
        in_specs=_prenorm_specs(tm, k) + [_stationary_weight_spec(k, tn, layer)],
        out_specs=pl.BlockSpec((tm, tn), lambda j, i: (i, j)),
        out_shape=jax.ShapeDtypeStruct((m, n), out_dtype),
        compiler_params=_params("parallel", "arbitrary"),
        name="matmul_gelu" if gelu else "matmul",
    )(x, ss, w)


def _mm_by_residue_kernel(x_ref, ss_ref, w_ref, o_ref, acc_ref, *, dilation):
    acc = jnp.dot(x_ref[...], w_ref[...].astype(BF16), preferred_element_type=F32)
    acc = acc * _row_rsqrt(ss_ref, x_ref.shape[1])
    rows = acc.shape[0] // dilation
    for c in range(acc_ref.shape[0]):
        lanes = slice(c * LANES, (c + 1) * LANES)
        acc_ref[c] = acc[:, lanes]
        for r in range(dilation):
            o_ref[r, :, lanes] = acc_ref[c, pl.ds(r, rows, stride=dilation), :].astype(o_ref.dtype)


def matmul_by_residue(x, ss, w, layer, group, n_groups, dilation, *, tm=512, tn=1024):
    m, k = x.shape
    n = w.shape[2] // n_groups
    tm, tn = _tile(m, tm), _tile(n, tn)
    return pl.pallas_call(
        functools.partial(_mm_by_residue_kernel, dilation=dilation),
        grid=(n // tn, m // tm),
        in_specs=_prenorm_specs(tm, k) + [_stationary_weight_spec(k, tn, layer, group * (n // tn))],
        out_specs=pl.BlockSpec((dilation, tm // dilation, tn), lambda j, i: (0, i, j)),
        out_shape=jax.ShapeDtypeStruct((dilation, m // dilation, n), BF16),
        scratch_shapes=[pltpu.VMEM((tn // LANES, tm, LANES), F32)],
        compiler_params=_params("parallel", "arbitrary"),
        name=f"matmul_by_residue_d{dilation}",
    )(x, ss, w)


def _mm_residual_kernel(x_ref, w_ref, r_ref, gn_ref, o_ref, ob_ref, ss_ref):
    h = r_ref[...] + jnp.dot(x_ref[...], w_ref[...], preferred_element_type=F32)
    o_ref[...] = h
    ob_ref[...] = (h * gn_ref[...]).astype(BF16)
    part = jnp.sum(h * h, axis=-1, keepdims=True)

    @pl.when(pl.program_id(1) == 0)
    def _():
        ss_ref[...] = part

    @pl.when(pl.program_id(1) > 0)
    def _():
        ss_ref[...] += part


def matmul_residual(x, w, layer, res, next_gain):
    m, k = x.shape
    n = w.shape[2]
    tm, tn = (_tile(m, 1024), _tile(n, 512)) if k <= 4096 else (_tile(m, 512), _tile(n, 512))
    return pl.pallas_call(
        _mm_residual_kernel,
        grid=(m // tm, n // tn),
        in_specs=[
            pl.BlockSpec((tm, k), lambda i, j: (i, 0)),
            _weight_spec(k, tn, layer),
            pl.BlockSpec((tm, tn), lambda i, j: (i, j)),
            pl.BlockSpec((1, tn), lambda i, j: (0, j)),
        ],
        out_specs=[pl.BlockSpec((tm, tn), lambda i, j: (i, j)), pl.BlockSpec((tm, tn), lambda i, j: (i, j)),
                   pl.BlockSpec((tm, 1), lambda i, j: (i, 0))],
        out_shape=[jax.ShapeDtypeStruct((m, n), F32), jax.ShapeDtypeStruct((m, n), BF16),
                   jax.ShapeDtypeStruct((m, 1), F32)],
        compiler_params=_params("parallel", "arbitrary"),
        name="matmul_residual",
    )(x, w, res, next_gain.reshape(1, n))


def _gate_up_kernel(x_ref, ss_ref, wg_hbm, wu_hbm, o_ref, wg_buf, wu_buf, sems, *, layer, tn):
    j, i = pl.program_id(0), pl.program_id(1)
    slot = j % 2

    def copies(block, into):
        cols = pl.ds(pl.multiple_of(block * tn, tn), tn)
        return (pltpu.make_async_copy(wg_hbm.at[layer, :, cols], wg_buf.at[into], sems.at[0, into]),
                pltpu.make_async_copy(wu_hbm.at[layer, :, cols], wu_buf.at[into], sems.at[1, into]))

    @pl.when(i == 0)
    def _():
        @pl.when(j == 0)
        def _():
            for copy in copies(0, 0):
                copy.start()

        for copy in copies(j, slot):
            copy.wait()

        @pl.when(j + 1 < pl.num_programs(0))
        def _():
            for copy in copies(j + 1, 1 - slot):
                copy.start()

    x = x_ref[...]
    r = _row_rsqrt(ss_ref, x_ref.shape[1])
    g = jnp.dot(x, wg_buf[slot].astype(BF16), preferred_element_type=F32) * r
    u = jnp.dot(x, wu_buf[slot].astype(BF16), preferred_element_type=F32) * r
    o_ref[...] = (g * jax.nn.sigmoid(g) * u).astype(o_ref.dtype)


def gate_up(x, ss, wg, wu, layer, *, tm=1024, tn=256):
    m, k = x.shape
    n = wg.shape[2]
    tm, tn = _tile(m, tm), _tile(n, tn)
    return pl.pallas_call(
        functools.partial(_gate_up_kernel, layer=layer, tn=tn),
        grid=(n // tn, m // tm),
        in_specs=_prenorm_specs(tm, k) + [pl.BlockSpec(memory_space=pl.ANY)] * 2,
        out_specs=pl.BlockSpec((tm, tn), lambda j, i: (i, j)),
        out_shape=jax.ShapeDtypeStruct((m, n), BF16),
        scratch_shapes=[pltpu.VMEM((2, k, tn), F32), pltpu.VMEM((2, k, tn), F32),
                        pltpu.SemaphoreType.DMA((2, 2))],
        compiler_params=_params("arbitrary", "arbitrary"),
        name="gate_up",
    )(x, ss, wg, wu)


def _sgu_kernel(u_ref, v_ref, gv_ref, ws_ref, bs_ref, o_ref, *, n_groups, chunks):
    v = v_ref[...]
    ms = jnp.mean(v * v, axis=-1, keepdims=True)
    vn = (v * lax.rsqrt(ms + EPS) * gv_ref[...]).astype(BF16)
    row = lax.broadcasted_iota(jnp.int32, (SGU_CHUNK, SGU_CHUNK), 0)
    col = lax.broadcasted_iota(jnp.int32, (SGU_CHUNK, SGU_CHUNK), 1)
    bs = bs_ref[...]
    for g in range(n_groups):
        w = jnp.where(col <= row, ws_ref[g], 0.0).astype(BF16)
        b = bs[:, g : g + 1]
        lanes = slice(g * HEAD_DIM, (g + 1) * HEAD_DIM)
        for c in range(chunks):
            rows = slice(c * SGU_CHUNK, (c + 1) * SGU_CHUNK)
            mixed = jnp.dot(w, vn[rows, lanes], preferred_element_type=F32) + b
            o_ref[rows, lanes] = (u_ref[rows, lanes] * mixed).astype(o_ref.dtype)


def sgu_gate(z, g_v, w_s, b_s):
    s, two_d = z.shape
    d = two_d // 2
    n_groups = d // HEAD_DIM
    chunks = 2 if s % (2 * SGU_CHUNK) == 0 else 1
    tr = chunks * SGU_CHUNK
    return pl.pallas_call(
        functools.partial(_sgu_kernel, n_groups=n_groups, chunks=chunks),
        grid=(s // tr,),
        in_specs=[
            pl.BlockSpec((tr, d), lambda i: (i, 0)),
            pl.BlockSpec((tr, d), lambda i: (i, 1)),
            pl.BlockSpec((1, d), lambda i: (0, 0)),
            pl.BlockSpec((n_groups, SGU_CHUNK, SGU_CHUNK), lambda i: (0, 0, 0)),
            pl.BlockSpec((SGU_CHUNK, n_groups), lambda i: (0, 0)),
        ],
        out_specs=pl.BlockSpec((tr, d), lambda i: (i, 0)),
        out_shape=jax.ShapeDtypeStruct((s, d), BF16),
        compiler_params=_params("parallel"),
        name="sgu_gate",
    )(z, z, g_v.reshape(1, d), w_s, b_s.T)


def _sb_kernel(q_ref, k_ref, v_ref, o_ref, acc_ref, *, tb, heads, scale_log2):
    i = pl.program_id(1)
    row = lax.broadcasted_iota(jnp.int32, (tb, tb), 0)
    col = lax.broadcasted_iota(jnp.int32, (tb, tb), 1)
    neg_later = jnp.where(row > col, -1.0, 0.0).astype(BF16)
    causal = col < row
    dims = (((1,), (1,)), ((), ()))

    def block(j, carries, diagonal):
        start = pl.multiple_of(j * tb, tb)
        chains = range(heads)
        lanes = [slice(c * HEAD_DIM, (c + 1) * HEAD_DIM) for c in chains]
        z2 = [lax.dot_general(q_ref[:, lanes[c]], k_ref[pl.ds(start, tb), lanes[c]], dims,
                              preferred_element_type=F32) * scale_log2 for c in chains]
        log2_beta, parts, new_carries = [], [], []
        for c in chains:
            sp2 = jnp.maximum(z2[c], 0.0) + jnp.log(1.0 + jnp.exp2(-jnp.abs(z2[c]))) * LOG2_E
            log2_beta.append(z2[c] - sp2)
            if diagonal:
                sp2 = jnp.where(causal, sp2, 0.0)
            hi = pltpu.bitcast(pltpu.bitcast(sp2, jnp.uint32) & jnp.uint32(0xFFFF0000), F32)
            parts.append(jnp.concatenate([hi.astype(BF16), (sp2 - hi).astype(BF16)], axis=0))
            new_carries.append(carries[c] - jnp.sum(sp2, axis=-1, keepdims=True))
        both = [jnp.dot(parts[c], neg_later, preferred_element_type=F32) for c in chains]
        a = []
        for c in chains:
            log2_tail = both[c][:tb] + both[c][tb:] + carries[c]
            a_c = jnp.exp2(log2_beta[c] + log2_tail)
            if diagonal:
                a_c = jnp.where(causal, a_c, 0.0)
            a.append(a_c.astype(BF16))
        for c in chains:
            av = jnp.dot(a[c], v_ref[pl.ds(start, tb), lanes[c]], preferred_element_type=F32)
            if diagonal:
                acc_ref[c] = av
            else:
                acc_ref[c] += av
        return tuple(new_carries)

    def any_live(carries):
        return functools.reduce(jnp.maximum, [jnp.max(c) for c in carries]) > EXP2_TO_ZERO

    def body(state):
        it, carries, _ = state
        carries = block(i - 1 - it, carries, False)
        return it + 1, carries, any_live(carries)

    carries = block(i, tuple(jnp.zeros((tb, 1), F32) for _ in range(heads)), True)
    lax.while_loop(lambda state: jnp.logical_and(state[0] < i, state[2]), body,
                   (jnp.int32(0), carries, any_live(carries)))
    for c in range(heads):
        o_ref[:, c * HEAD_DIM : (c + 1) * HEAD_DIM] = acc_ref[c].astype(o_ref.dtype)


def stick_breaking(qkv, n_heads):
    s = qkv.shape[0]
    tb = _tile(s, 256)
    heads = _tile(n_heads, 4)
    groups = n_heads // heads
    width = heads * HEAD_DIM
    return pl.pallas_call(
        functools.partial(_sb_kernel, tb=tb, heads=heads, scale_log2=LOG2_E / math.sqrt(HEAD_DIM)),
        grid=(groups, s // tb),
        in_specs=[
            pl.BlockSpec((tb, width), lambda h, i: (i, h)),
            pl.BlockSpec((s, width), lambda h, i: (0, groups + h)),
            pl.BlockSpec((s, width), lambda h, i: (0, 2 * groups + h)),
        ],
        out_specs=pl.BlockSpec((tb, width), lambda h, i: (i, h)),
        out_shape=jax.ShapeDtypeStruct((s, n_heads * HEAD_DIM), BF16),
        scratch_shapes=[pltpu.VMEM((heads, tb, HEAD_DIM), F32)],
        compiler_params=_params("parallel", "arbitrary"),
        name="stick_breaking",
    )(qkv, qkv, qkv)


def _dilated_kernel(q_ref, kp_ref, kc_ref, vp_ref, vc_ref, o_ref, lse_ref, *, n_heads, band, scale):
    has_prev = pl.program_id(1) > 0
    qi = lax.broadcasted_iota(jnp.int32, (band, band), 0)
    kj = lax.broadcasted_iota(jnp.int32, (band, band), 1)
    valid_prev = (kj >= qi) & has_prev
    valid_cur = kj <= qi
    lane = lax.broadcasted_iota(jnp.int32, (band, LANES), 1)
    dims = (((1,), (1,)), ((), ()))
    lse_all = jnp.zeros((band, LANES), F32)
    chunk = _tile(n_heads, 4)
    for h0 in range(0, n_heads, chunk):
        heads = range(h0, h0 + chunk)
        lanes = {h: slice(h * HEAD_DIM, (h + 1) * HEAD_DIM) for h in heads}
        zp = {h: lax.dot_general(q_ref[:, lanes[h]], kp_ref[:, lanes[h]], dims, preferred_element_type=F32)
              for h in heads}
        zc = {h: lax.dot_general(q_ref[:, lanes[h]], kc_ref[:, lanes[h]], dims, preferred_element_type=F32)
              for h in heads}
        wp, wc = {}, {}
        for h in heads:
            zp_h = jnp.where(valid_prev, zp[h] * scale, -jnp.inf)
            zc_h = jnp.where(valid_cur, zc[h] * scale, -jnp.inf)
            m = jnp.max(jnp.maximum(zp_h, zc_h), axis=-1, keepdims=True)
            pp = jnp.exp(zp_h - m)
            pc = jnp.exp(zc_h - m)
            den = jnp.sum(pp + pc, axis=-1, keepdims=True)
            inv = 1.0 / den
            wp[h] = (pp * inv).astype(BF16)
            wc[h] = (pc * inv).astype(BF16)
            lse_all = jnp.where(lane == h, m + jnp.log(den), lse_all)
        for h in heads:
            o_ref[:, lanes[h]] = (jnp.dot(wp[h], vp_ref[:, lanes[h]], preferred_element_type=F32)
                                  + jnp.dot(wc[h], vc_ref[:, lanes[h]], preferred_element_type=F32))
    lse_ref[...] = lse_all


def dilated_group(qkv, n_heads, band):
    dilation, n, _ = qkv.shape
    assert n_heads <= LANES
    width = n_heads * HEAD_DIM
    blk = (None, band, width)
    cur = lambda which: pl.BlockSpec(blk, lambda r, b: (r, b, which))
    prev = lambda which: pl.BlockSpec(blk, lambda r, b: (r, jnp.maximum(b - 1, 0), which))
    return pl.pallas_call(
        functools.partial(_dilated_kernel, n_heads=n_heads, band=band, scale=1.0 / math.sqrt(HEAD_DIM)),
        grid=(dilation, n // band),
        in_specs=[cur(0), prev(1), cur(1), prev(2), cur(2)],
        out_specs=[
            pl.BlockSpec(blk, lambda r, b: (r, b, 0)),
            pl.BlockSpec((None, band, LANES), lambda r, b: (r, b, 0)),
        ],
        out_shape=[
            jax.ShapeDtypeStruct((dilation, n, width), F32),
            jax.ShapeDtypeStruct((dilation, n, LANES), F32),
        ],
        compiler_params=_params("parallel", "arbitrary"),
        name=f"dilated_d{dilation}",
    )(qkv, qkv, qkv, qkv, qkv)


def _mix_kernel(*refs, dilations, n_heads):
    n_groups = len(dilations)
    o_refs, lse_refs = refs[:n_groups], refs[n_groups : 2 * n_groups]
    out_ref, o_seq, lse_seq = refs[2 * n_groups :]
    tr = out_ref.shape[0]
    for g, d in enumerate(dilations):
        for r in range(d):
            rows = pl.ds(r, tr // d, stride=d)
            lse_seq[g, rows, :] = lse_refs[g][r]
            for h in range(n_heads):
                o_seq[g, h, rows, :] = o_refs[g][r, :, h * HEAD_DIM : (h + 1) * HEAD_DIM]
    lses = [lse_seq[g] for g in range(n_groups)]
    m = functools.reduce(jnp.maximum, lses)
    es = [jnp.exp(l - m) for l in lses]
    total = functools.reduce(jnp.add, es)
    alphas = [e / total for e in es]
    for h in range(n_heads):
        acc = alphas[0][:, h : h + 1] * o_seq[0, h]
        for g in range(1, n_groups):
            acc = acc + alphas[g][:, h : h + 1] * o_seq[g, h]
        out_ref[:, h * HEAD_DIM : (h + 1) * HEAD_DIM] = acc.astype(out_ref.dtype)


def mix_groups(outs, lses):
    n_groups = len(outs)
    dilations = tuple(o.shape[0] for o in outs)
    width = outs[0].shape[2]
    n_heads = width // HEAD_DIM
    s = outs[0].shape[0] * outs[0].shape[1]
    tr = _tile(s, 256)
    return pl.pallas_call(
        functools.partial(_mix_kernel, dilations=dilations, n_heads=n_heads),
        grid=(s // tr,),
        in_specs=[pl.BlockSpec((d, tr // d, width), lambda i: (0, i, 0)) for d in dilations]
        + [pl.BlockSpec((d, tr // d, LANES), lambda i: (0, i, 0)) for d in dilations],
        out_specs=pl.BlockSpec((tr, width), lambda i: (i, 0)),
        out_shape=jax.ShapeDtypeStruct((s, width), BF16),
        scratch_shapes=[pltpu.VMEM((n_groups, n_heads, tr, HEAD_DIM), F32), pltpu.VMEM((n_groups, tr, LANES), F32)],
        compiler_params=_params("parallel"),
        name="mix_groups",
    )(*outs, *lses)


def kernel(x, norm_mix, norm_ffn, norm_final, a_w_in, a_g_v, a_w_s, a_b_s, a_w_out,
           b_w_qkv, b_w_o, c_w_qkv, c_w_o, w_gate, w_up, w_down):
    batch, seq, d_model = x.shape
    depth = norm_mix.shape[0]
    n_dil = len(DIL_PAIRS)
    a_w_out, b_w_o, c_w_o, w_down = (w.astype(BF16) for w in (a_w_out, b_w_o, c_w_o, w_down))
    outs = []
    for b in range(batch):
        h = x[b]
        hb, ss = scale_with_row_stats(h, norm_mix[0])
        for i in range(depth):
            kind, j = i % N_MIXERS, i // N_MIXERS
            if kind == 0:
                z = matmul(hb, ss, a_w_in, j, F32, gelu=True)
                gated = sgu_gate(z, a_g_v[j], a_w_s[j], a_b_s[j])
                h, hb, ss = matmul_residual(gated, a_w_out, j, h, norm_ffn[i])
            elif kind == 1:
                n_heads = b_w_o.shape[1] // HEAD_DIM
                qkv = matmul(hb, ss, b_w_qkv, j, BF16)
                o = stick_breaking(qkv, n_heads)
                h, hb, ss = matmul_residual(o, b_w_o, j, h, norm_ffn[i])
            else:
                n_heads = c_w_o.shape[1] // HEAD_DIM
                parts = []
                for g, (window, dilation) in enumerate(DIL_PAIRS):
                    qkv = matmul_by_residue(hb, ss, c_w_qkv, j, g, n_dil, dilation)
                    parts.append(dilated_group(qkv, n_heads, window // dilation))
                o = mix_groups([p[0] for p in parts], [p[1] for p in parts])
                h, hb, ss = matmul_residual(o, c_w_o, j, h, norm_ffn[i])
            act = gate_up(hb, ss, w_gate, w_up, i)
            next_gain = norm_mix[i + 1] if i + 1 < depth else norm_final
            h, hb, ss = matmul_residual(act, w_down, i, h, next_gain)
        outs.append(rmsnorm(h, norm_final, F32))
    return jnp.stack(outs, axis=0)
```

```python
import functools
import math

import jax
import jax.numpy as jnp
from jax import lax
from jax.experimental import pallas as pl
from jax.experimental.pallas import tpu as pltpu

EPS = 1e-6
LANES = 128
HEAD_DIM = 128
SGU_CHUNK = 128
DIL_PAIRS = ((128, 1), (512, 4), (2048, 16))
N_MIXERS = 3

LOG2_E = 1.4426950408889634
EXP2_TO_ZERO = -151.0

F32 = jnp.float32
BF16 = jnp.bfloat16

VMEM_LIMIT_BYTES = 56 * 1024 * 1024


def _params(*semantics):
    return pltpu.CompilerParams(dimension_semantics=semantics, vmem_limit_bytes=VMEM_LIMIT_BYTES)


def _tile(dim, preferred):
    t = min(dim, preferred)
    while dim % t:
        t //= 2
    return t


def _rmsnorm_kernel(x_ref, g_ref, o_ref):
    x = x_ref[...]
    ms = jnp.mean(x * x, axis=-1, keepdims=True)
    o_ref[...] = (x * lax.rsqrt(ms + EPS) * g_ref[...]).astype(o_ref.dtype)


def rmsnorm(x, g, out_dtype):
    s, d = x.shape
    tr = _tile(s, 256)
    return pl.pallas_call(
        _rmsnorm_kernel,
        grid=(s // tr,),
        in_specs=[pl.BlockSpec((tr, d), lambda i: (i, 0)), pl.BlockSpec((1, d), lambda i: (0, 0))],
        out_specs=pl.BlockSpec((tr, d), lambda i: (i, 0)),
        out_shape=jax.ShapeDtypeStruct((s, d), out_dtype),
        compiler_params=_params("parallel"),
        name="rmsnorm",
    )(x, g.reshape(1, d))


def _cast_stats_kernel(x_ref, g_ref, b_ref, s_ref):
    x = x_ref[...]
    b_ref[...] = (x * g_ref[...]).astype(BF16)
    s_ref[...] = jnp.sum(x * x, axis=-1, keepdims=True)


def scale_with_row_stats(x, gain):
    s, d = x.shape
    tr = _tile(s, 256)
    return pl.pallas_call(
        _cast_stats_kernel,
        grid=(s // tr,),
        in_specs=[pl.BlockSpec((tr, d), lambda i: (i, 0)), pl.BlockSpec((1, d), lambda i: (0, 0))],
        out_specs=[pl.BlockSpec((tr, d), lambda i: (i, 0)), pl.BlockSpec((tr, 1), lambda i: (i, 0))],
        out_shape=[jax.ShapeDtypeStruct((s, d), BF16), jax.ShapeDtypeStruct((s, 1), F32)],
        compiler_params=_params("parallel"),
        name="scale_with_row_stats",
    )(x, gain.reshape(1, d))


def _gelu_exact(x):
    return 0.5 * x * (1.0 + lax.erf(x * (1.0 / math.sqrt(2.0))))


def _row_rsqrt(ss_ref, k):
    return lax.rsqrt(ss_ref[...] * (1.0 / k) + EPS)


def _prenorm_specs(tm, k):
    return [pl.BlockSpec((tm, k), lambda j, i: (i, 0)), pl.BlockSpec((tm, 1), lambda j, i: (i, 0))]


def _mm_kernel(x_ref, ss_ref, w_hbm, o_ref, w_buf, sems, *, gelu, layer, tn):
    slot = _prefetch_weight_block([(w_hbm, w_buf)], sems, layer, 0, tn)
    acc = jnp.dot(x_ref[...], w_buf[slot].astype(BF16), preferred_element_type=F32)
    acc = acc * _row_rsqrt(ss_ref, x_ref.shape[1])
    if gelu:
        acc = _gelu_exact(acc)
    o_ref[...] = acc.astype(o_ref.dtype)


def _weight_spec(k, tn, layer):
    return pl.BlockSpec((None, k, tn), lambda i, j: (layer, 0, j))


def _prefetch_weight_block(weights, sems, layer, col_block0, tn):
    j, i = pl.program_id(0), pl.program_id(1)
    slot = j % 2

    def copies(block, into):
        cols = pl.ds(pl.multiple_of((col_block0 + block) * tn, tn), tn)
        return [pltpu.make_async_copy(hbm.at[layer, :, cols], buf.at[into], sems.at[n, into])
                for n, (hbm, buf) in enumerate(weights)]

    @pl.when(i == 0)
    def _():
        @pl.when(j == 0)
        def _():
            for copy in copies(0, 0):
                copy.start()

        for copy in copies(j, slot):
            copy.wait()

        @pl.when(j + 1 < pl.num_programs(0))
        def _():
            for copy in copies(j + 1, 1 - slot):
                copy.start()

    return slot


def _weight_prefetch_scratch(n_weights, k, tn):
    return [pltpu.VMEM((2, k, tn), F32)] * n_weights + [pltpu.SemaphoreType.DMA((n_weights, 2))]


def matmul(x, ss, w, layer, out_dtype, *, gelu=False, tm=512, tn=1024):
    m, k = x.shape
    n = w.shape[2]
    tm, tn = _tile(m, tm), _tile(n, tn)
    return pl.pallas_call(
        functools.partial(_mm_kernel, gelu=gelu, layer=layer, tn=tn),
        grid=(n // tn, m // tm),
        in_specs=_prenorm_specs(tm, k) + [pl.BlockSpec(memory_space=pl.ANY)],
        out_specs=pl.BlockSpec((tm, tn), lambda j, i: (i, j)),
        out_shape=jax.ShapeDtypeStruct((m, n), out_dtype),
        scratch_shapes=_weight_prefetch_scratch(1, k, tn),
        compiler_params=_params("arbitrary", "arbitrary"),
        name="matmul_gelu" if gelu else "matmul",
    )(x, ss, w)


def _mm_by_residue_kernel(x_ref, ss_ref, w_hbm, o_ref, acc_ref, w_buf, sems, *, dilation, layer, col_block0, tn):
    slot = _prefetch_weight_block([(w_hbm, w_buf)], sems, layer, col_block0, tn)
    acc = jnp.dot(x_ref[...], w_buf[slot].astype(BF16), preferred_element_type=F32)
    acc = acc * _row_rsqrt(ss_ref, x_ref.shape[1])
    rows = acc.shape[0] // dilation
    for c in range(acc_ref.shape[0]):
        lanes = slice(c * LANES, (c + 1) * LANES)
        acc_ref[c] = acc[:, lanes]
        for r in range(dilation):
            o_ref[r, :, lanes] = acc_ref[c, pl.ds(r, rows, stride=dilation), :].astype(o_ref.dtype)


def matmul_by_residue(x, ss, w, layer, group, n_groups, dilation, *, tm=512, tn=1024):
    m, k = x.shape
    n = w.shape[2] // n_groups
    tm, tn = _tile(m, tm), _tile(n, tn)
    return pl.pallas_call(
        functools.partial(_mm_by_residue_kernel, dilation=dilation, layer=layer, col_block0=group * (n // tn), tn=tn),
        grid=(n // tn, m // tm),
        in_specs=_prenorm_specs(tm, k) + [pl.BlockSpec(memory_space=pl.ANY)],
        out_specs=pl.BlockSpec((dilation, tm // dilation, tn), lambda j, i: (0, i, j)),
        out_shape=jax.ShapeDtypeStruct((dilation, m // dilation, n), BF16),
        scratch_shapes=[pltpu.VMEM((tn // LANES, tm, LANES), F32)] + _weight_prefetch_scratch(1, k, tn),
        compiler_params=_params("arbitrary", "arbitrary"),
        name=f"matmul_by_residue_d{dilation}",
    )(x, ss, w)


def _mm_residual_kernel(x_ref, w_ref, r_ref, gn_ref, o_ref, ob_ref, ss_ref):
    h = r_ref[...] + jnp.dot(x_ref[...], w_ref[...], preferred_element_type=F32)
    o_ref[...] = h
    ob_ref[...] = (h * gn_ref[...]).astype(BF16)
    part = jnp.sum(h * h, axis=-1, keepdims=True)

    @pl.when(pl.program_id(1) == 0)
    def _():
        ss_ref[...] = part

    @pl.when(pl.program_id(1) > 0)
    def _():
        ss_ref[...] += part


def matmul_residual(x, w, layer, res, next_gain):
    m, k = x.shape
    n = w.shape[2]
    tm, tn = (_tile(m, 1024), _tile(n, 512)) if k <= 4096 else (_tile(m, 512), _tile(n, 512))
    return pl.pallas_call(
        _mm_residual_kernel,
        grid=(m // tm, n // tn),
        in_specs=[
            pl.BlockSpec((tm, k), lambda i, j: (i, 0)),
            _weight_spec(k, tn, layer),
            pl.BlockSpec((tm, tn), lambda i, j: (i, j)),
            pl.BlockSpec((1, tn), lambda i, j: (0, j)),
        ],
        out_specs=[pl.BlockSpec((tm, tn), lambda i, j: (i, j)), pl.BlockSpec((tm, tn), lambda i, j: (i, j)),
                   pl.BlockSpec((tm, 1), lambda i, j: (i, 0))],
        out_shape=[jax.ShapeDtypeStruct((m, n), F32), jax.ShapeDtypeStruct((m, n), BF16),
                   jax.ShapeDtypeStruct((m, 1), F32)],
        compiler_params=_params("parallel", "arbitrary"),
        name="matmul_residual",
    )(x, w, res, next_gain.reshape(1, n))


def _gate_up_kernel(x_ref, ss_ref, wg_hbm, wu_hbm, o_ref, wg_buf, wu_buf, sems, *, layer, tn):
    slot = _prefetch_weight_block([(wg_hbm, wg_buf), (wu_hbm, wu_buf)], sems, layer, 0, tn)
    x = x_ref[...]
    r = _row_rsqrt(ss_ref, x_ref.shape[1])
    g = jnp.dot(x, wg_buf[slot].astype(BF16), preferred_element_type=F32) * r
    u = jnp.dot(x, wu_buf[slot].astype(BF16), preferred_element_type=F32) * r
    o_ref[...] = (g * jax.nn.sigmoid(g) * u).astype(o_ref.dtype)


def gate_up(x, ss, wg, wu, layer, *, tm=1024, tn=256):
    m, k = x.shape
    n = wg.shape[2]
    tm, tn = _tile(m, tm), _tile(n, tn)
    return pl.pallas_call(
        functools.partial(_gate_up_kernel, layer=layer, tn=tn),
        grid=(n // tn, m // tm),
        in_specs=_prenorm_specs(tm, k) + [pl.BlockSpec(memory_space=pl.ANY)] * 2,
        out_specs=pl.BlockSpec((tm, tn), lambda j, i: (i, j)),
        out_shape=jax.ShapeDtypeStruct((m, n), BF16),
        scratch_shapes=_weight_prefetch_scratch(2, k, tn),
        compiler_params=_params("arbitrary", "arbitrary"),
        name="gate_up",
    )(x, ss, wg, wu)


def _sgu_kernel(u_ref, v_ref, gv_ref, ws_ref, bs_ref, o_ref, *, n_groups, chunks):
    v = v_ref[...]
    ms = jnp.mean(v * v, axis=-1, keepdims=True)
    vn = (v * lax.rsqrt(ms + EPS) * gv_ref[...]).astype(BF16)
    row = lax.broadcasted_iota(jnp.int32, (SGU_CHUNK, SGU_CHUNK), 0)
    col = lax.broadcasted_iota(jnp.int32, (SGU_CHUNK, SGU_CHUNK), 1)
    bs = bs_ref[...]
    for g in range(n_groups):
        w = jnp.where(col <= row, ws_ref[g], 0.0).astype(BF16)
        b = bs[:, g : g + 1]
        lanes = slice(g * HEAD_DIM, (g + 1) * HEAD_DIM)
        for c in range(chunks):
            rows = slice(c * SGU_CHUNK, (c + 1) * SGU_CHUNK)
            mixed = jnp.dot(w, vn[rows, lanes], preferred_element_type=F32) + b
            o_ref[rows, lanes] = (u_ref[rows, lanes] * mixed).astype(o_ref.dtype)


def sgu_gate(z, g_v, w_s, b_s):
    s, two_d = z.shape
    d = two_d // 2
    n_groups = d // HEAD_DIM
    chunks = 2 if s % (2 * SGU_CHUNK) == 0 else 1
    tr = chunks * SGU_CHUNK
    return pl.pallas_call(
        functools.partial(_sgu_kernel, n_groups=n_groups, chunks=chunks),
        grid=(s // tr,),
        in_specs=[
            pl.BlockSpec((tr, d), lambda i: (i, 0)),
            pl.BlockSpec((tr, d), lambda i: (i, 1)),
            pl.BlockSpec((1, d), lambda i: (0, 0)),
            pl.BlockSpec((n_groups, SGU_CHUNK, SGU_CHUNK), lambda i: (0, 0, 0)),
            pl.BlockSpec((SGU_CHUNK, n_groups), lambda i: (0, 0)),
        ],
        out_specs=pl.BlockSpec((tr, d), lambda i: (i, 0)),
        out_shape=jax.ShapeDtypeStruct((s, d), BF16),
        compiler_params=_params("parallel"),
        name="sgu_gate",
    )(z, z, g_v.reshape(1, d), w_s, b_s.T)


def _sb_kernel(q_ref, k_ref, v_ref, o_ref, acc_ref, *, tb, heads, scale_log2):
    i = pl.program_id(1)
    row = lax.broadcasted_iota(jnp.int32, (tb, tb), 0)
    col = lax.broadcasted_iota(jnp.int32, (tb, tb), 1)
    neg_later = jnp.where(row > col, -1.0, 0.0).astype(BF16)
    causal = col < row
    dims = (((1,), (1,)), ((), ()))

    def block(j, carries, diagonal):
        start = pl.multiple_of(j * tb, tb)
        chains = range(heads)
        lanes = [slice(c * HEAD_DIM, (c + 1) * HEAD_DIM) for c in chains]
        z2 = [lax.dot_general(q_ref[:, lanes[c]], k_ref[pl.ds(start, tb), lanes[c]], dims,
                              preferred_element_type=F32) * scale_log2 for c in chains]
        log2_beta, parts, new_carries = [], [], []
        for c in chains:
            sp2 = jnp.maximum(z2[c], 0.0) + jnp.log(1.0 + jnp.exp2(-jnp.abs(z2[c]))) * LOG2_E
            log2_beta.append(z2[c] - sp2)
            if diagonal:
                sp2 = jnp.where(causal, sp2, 0.0)
            hi = pltpu.bitcast(pltpu.bitcast(sp2, jnp.uint32) & jnp.uint32(0xFFFF0000), F32)
            parts.append(jnp.concatenate([hi.astype(BF16), (sp2 - hi).astype(BF16)], axis=0))
            new_carries.append(carries[c] - jnp.sum(sp2, axis=-1, keepdims=True))
        both = [jnp.dot(parts[c], neg_later, preferred_element_type=F32) for c in chains]
        a = []
        for c in chains:
            log2_tail = both[c][:tb] + both[c][tb:] + carries[c]
            a_c = jnp.exp2(log2_beta[c] + log2_tail)
            if diagonal:
                a_c = jnp.where(causal, a_c, 0.0)
            a.append(a_c.astype(BF16))
        for c in chains:
            av = jnp.dot(a[c], v_ref[pl.ds(start, tb), lanes[c]], preferred_element_type=F32)
            if diagonal:
                acc_ref[c] = av
            else:
                acc_ref[c] += av
        return tuple(new_carries)

    def any_live(carries):
        return functools.reduce(jnp.maximum, [jnp.max(c) for c in carries]) > EXP2_TO_ZERO

    def body(state):
        it, carries, _ = state
        carries = block(i - 1 - it, carries, False)
        return it + 1, carries, any_live(carries)

    carries = block(i, tuple(jnp.zeros((tb, 1), F32) for _ in range(heads)), True)
    lax.while_loop(lambda state: jnp.logical_and(state[0] < i, state[2]), body,
                   (jnp.int32(0), carries, any_live(carries)))
    for c in range(heads):
        o_ref[:, c * HEAD_DIM : (c + 1) * HEAD_DIM] = acc_ref[c].astype(o_ref.dtype)


def stick_breaking(qkv, n_heads):
    s = qkv.shape[0]
    tb = _tile(s, 256)
    heads = _tile(n_heads, 4)
    groups = n_heads // heads
    width = heads * HEAD_DIM
    return pl.pallas_call(
        functools.partial(_sb_kernel, tb=tb, heads=heads, scale_log2=LOG2_E / math.sqrt(HEAD_DIM)),
        grid=(groups, s // tb),
        in_specs=[
            pl.BlockSpec((tb, width), lambda h, i: (i, h)),
            pl.BlockSpec((s, width), lambda h, i: (0, groups + h)),
            pl.BlockSpec((s, width), lambda h, i: (0, 2 * groups + h)),
        ],
        out_specs=pl.BlockSpec((tb, width), lambda h, i: (i, h)),
        out_shape=jax.ShapeDtypeStruct((s, n_heads * HEAD_DIM), BF16),
        scratch_shapes=[pltpu.VMEM((heads, tb, HEAD_DIM), F32)],
        compiler_params=_params("parallel", "arbitrary"),
        name="stick_breaking",
    )(qkv, qkv, qkv)


def _dilated_kernel(q_ref, kp_ref, kc_ref, vp_ref, vc_ref, o_ref, lse_ref, *, n_heads, band, scale):
    has_prev = pl.program_id(1) > 0
    qi = lax.broadcasted_iota(jnp.int32, (band, band), 0)
    kj = lax.broadcasted_iota(jnp.int32, (band, band), 1)
    valid_prev = (kj >= qi) & has_prev
    valid_cur = kj <= qi
    lane = lax.broadcasted_iota(jnp.int32, (band, LANES), 1)
    dims = (((1,), (1,)), ((), ()))
    lse_all = jnp.zeros((band, LANES), F32)
    chunk = _tile(n_heads, 4)
    for h0 in range(0, n_heads, chunk):
        heads = range(h0, h0 + chunk)
        lanes = {h: slice(h * HEAD_DIM, (h + 1) * HEAD_DIM) for h in heads}
        zp = {h: lax.dot_general(q_ref[:, lanes[h]], kp_ref[:, lanes[h]], dims, preferred_element_type=F32)
              for h in heads}
        zc = {h: lax.dot_general(q_ref[:, lanes[h]], kc_ref[:, lanes[h]], dims, preferred_element_type=F32)
              for h in heads}
        wp, wc = {}, {}
        for h in heads:
            zp_h = jnp.where(valid_prev, zp[h] * scale, -jnp.inf)
            zc_h = jnp.where(valid_cur, zc[h] * scale, -jnp.inf)
            m = jnp.max(jnp.maximum(zp_h, zc_h), axis=-1, keepdims=True)
            pp = jnp.exp(zp_h - m)
            pc = jnp.exp(zc_h - m)
            den = jnp.sum(pp + pc, axis=-1, keepdims=True)
            inv = 1.0 / den
            wp[h] = (pp * inv).astype(BF16)
            wc[h] = (pc * inv).astype(BF16)
            lse_all = jnp.where(lane == h, m + jnp.log(den), lse_all)
        for h in heads:
            o_ref[:, lanes[h]] = (jnp.dot(wp[h], vp_ref[:, lanes[h]], preferred_element_type=F32)
                                  + jnp.dot(wc[h], vc_ref[:, lanes[h]], preferred_element_type=F32))
    lse_ref[...] = lse_all


def dilated_group(qkv, n_heads, band):
    dilation, n, _ = qkv.shape
    assert n_heads <= LANES
    width = n_heads * HEAD_DIM
    blk = (None, band, width)
    cur = lambda which: pl.BlockSpec(blk, lambda r, b: (r, b, which))
    prev = lambda which: pl.BlockSpec(blk, lambda r, b: (r, jnp.maximum(b - 1, 0), which))
    return pl.pallas_call(
        functools.partial(_dilated_kernel, n_heads=n_heads, band=band, scale=1.0 / math.sqrt(HEAD_DIM)),
        grid=(dilation, n // band),
        in_specs=[cur(0), prev(1), cur(1), prev(2), cur(2)],
        out_specs=[
            pl.BlockSpec(blk, lambda r, b: (r, b, 0)),
            pl.BlockSpec((None, band, LANES), lambda r, b: (r, b, 0)),
        ],
        out_shape=[
            jax.ShapeDtypeStruct((dilation, n, width), F32),
            jax.ShapeDtypeStruct((dilation, n, LANES), F32),
        ],
        compiler_params=_params("parallel", "arbitrary"),
        name=f"dilated_d{dilation}",
    )(qkv, qkv, qkv, qkv, qkv)


def _mix_kernel(*refs, dilations, n_heads):
    n_groups = len(dilations)
    o_refs, lse_refs = refs[:n_groups], refs[n_groups : 2 * n_groups]
    out_ref, o_seq, lse_seq = refs[2 * n_groups :]
    tr = out_ref.shape[0]
    for g, d in enumerate(dilations):
        for r in range(d):
            rows = pl.ds(r, tr // d, stride=d)
            lse_seq[g, rows, :] = lse_refs[g][r]
            for h in range(n_heads):
                o_seq[g, h, rows, :] = o_refs[g][r, :, h * HEAD_DIM : (h + 1) * HEAD_DIM]
    lses = [lse_seq[g] for g in range(n_groups)]
    m = functools.reduce(jnp.maximum, lses)
    es = [jnp.exp(l - m) for l in lses]
    total = functools.reduce(jnp.add, es)
    alphas = [e / total for e in es]
    for h in range(n_heads):
        acc = alphas[0][:, h : h + 1] * o_seq[0, h]
        for g in range(1, n_groups):
            acc = acc + alphas[g][:, h : h + 1] * o_seq[g, h]
        out_ref[:, h * HEAD_DIM : (h + 1) * HEAD_DIM] = acc.astype(out_ref.dtype)


def mix_groups(outs, lses):
    n_groups = len(outs)
    dilations = tuple(o.shape[0] for o in outs)
    width = outs[0].shape[2]
    n_heads = width // HEAD_DIM
    s = outs[0].shape[0] * outs[0].shape[1]
    tr = _tile(s, 256)
    return pl.pallas_call(
        functools.partial(_mix_kernel, dilations=dilations, n_heads=n_heads),
        grid=(s // tr,),
        in_specs=[pl.BlockSpec((d, tr // d, width), lambda i: (0, i, 0)) for d in dilations]
        + [pl.BlockSpec((d, tr // d, LANES), lambda i: (0, i, 0)) for d in dilations],
        out_specs=pl.BlockSpec((tr, width), lambda i: (i, 0)),
        out_shape=jax.ShapeDtypeStruct((s, width), BF16),
        scratch_shapes=[pltpu.VMEM((n_groups, n_heads, tr, HEAD_DIM), F32), pltpu.VMEM((n_groups, tr, LANES), F32)],
        compiler_params=_params("parallel"),
        name="mix_groups",
    )(*outs, *lses)


def kernel(x, norm_mix, norm_ffn, norm_final, a_w_in, a_g_v, a_w_s, a_b_s, a_w_out,
           b_w_qkv, b_w_o, c_w_qkv, c_w_o, w_gate, w_up, w_down):
    batch, seq, d_model = x.shape
    depth = norm_mix.shape[0]
    n_dil = len(DIL_PAIRS)
    a_w_out, b_w_o, c_w_o, w_down = (w.astype(BF16) for w in (a_w_out, b_w_o, c_w_o, w_down))
    outs = []
    for b in range(batch):
        h = x[b]
        hb, ss = scale_with_row_stats(h, norm_mix[0])
        for i in range(depth):
            kind, j = i % N_MIXERS, i // N_MIXERS
            if kind == 0:
                z = matmul(hb, ss, a_w_in, j, F32, gelu=True)
                gated = sgu_gate(z, a_g_v[j], a_w_s[j], a_b_s[j])
                h, hb, ss = matmul_residual(gated, a_w_out, j, h, norm_ffn[i])
            elif kind == 1:
                n_heads = b_w_o.shape[1] // HEAD_DIM
                qkv = matmul(hb, ss, b_w_qkv, j, BF16)
                o = stick_breaking(qkv, n_heads)
                h, hb, ss = matmul_residual(o, b_w_o, j, h, norm_ffn[i])
            else:
                n_heads = c_w_o.shape[1] // HEAD_DIM
                parts = []
                for g, (window, dilation) in enumerate(DIL_PAIRS):
                    qkv = matmul_by_residue(hb, ss, c_w_qkv, j, g, n_dil, dilation)
                    parts.append(dilated_group(qkv, n_heads, window // dilation))
                o = mix_groups([p[0] for p in parts], [p[1] for p in parts])
                h, hb, ss = matmul_residual(o, c_w_o, j, h, norm_ffn[i])
            act = gate_up(hb, ss, w_gate, w_up, i)
            next_gain = norm_mix[i + 1] if i + 1 < depth else norm_final
            h, hb, ss = matmul_residual(act, w_down, i, h, next_gain)
        outs.append(rmsnorm(h, norm_final, F32))
    return jnp.stack(outs, axis=0)
```

```python
import functools
import math

import jax
import jax.numpy as jnp
from jax import lax
from jax.experimental import pallas as pl
from jax.experimental.pallas import tpu as pltpu

EPS = 1e-6
LANES = 128
HEAD_DIM = 128
SGU_CHUNK = 128
DIL_PAIRS = ((128, 1), (512, 4), (2048, 16))
N_MIXERS = 3

LOG2_E = 1.4426950408889634
EXP2_TO_ZERO = -151.0

F32 = jnp.float32
BF16 = jnp.bfloat16

VMEM_LIMIT_BYTES = 56 * 1024 * 1024


def _params(*semantics):
    return pltpu.CompilerParams(dimension_semantics=semantics, vmem_limit_bytes=VMEM_LIMIT_BYTES)


def _tile(dim, preferred):
    t = min(dim, preferred)
    while dim % t:
        t //= 2
    return t


def _rmsnorm_kernel(x_ref, g_ref, o_ref):
    x = x_ref[...]
    ms = jnp.mean(x * x, axis=-1, keepdims=True)
    o_ref[...] = (x * lax.rsqrt(ms + EPS) * g_ref[...]).astype(o_ref.dtype)


def rmsnorm(x, g, out_dtype):
    s, d = x.shape
    tr = _tile(s, 256)
    return pl.pallas_call(
        _rmsnorm_kernel,
        grid=(s // tr,),
        in_specs=[pl.BlockSpec((tr, d), lambda i: (i, 0)), pl.BlockSpec((1, d), lambda i: (0, 0))],
        out_specs=pl.BlockSpec((tr, d), lambda i: (i, 0)),
        out_shape=jax.ShapeDtypeStruct((s, d), out_dtype),
        compiler_params=_params("parallel"),
        name="rmsnorm",
    )(x, g.reshape(1, d))


def _cast_stats_kernel(x_ref, g_ref, b_ref, s_ref):
    x = x_ref[...]
    b_ref[...] = (x * g_ref[...]).astype(BF16)
    s_ref[...] = jnp.sum(x * x, axis=-1, keepdims=True)


def scale_with_row_stats(x, gain):
    s, d = x.shape
    tr = _tile(s, 256)
    return pl.pallas_call(
        _cast_stats_kernel,
        grid=(s // tr,),
        in_specs=[pl.BlockSpec((tr, d), lambda i: (i, 0)), pl.BlockSpec((1, d), lambda i: (0, 0))],
        out_specs=[pl.BlockSpec((tr, d), lambda i: (i, 0)), pl.BlockSpec((tr, 1), lambda i: (i, 0))],
        out_shape=[jax.ShapeDtypeStruct((s, d), BF16), jax.ShapeDtypeStruct((s, 1), F32)],
        compiler_params=_params("parallel"),
        name="scale_with_row_stats",
    )(x, gain.reshape(1, d))


def _gelu_exact(x):
    return 0.5 * x * (1.0 + lax.erf(x * (1.0 / math.sqrt(2.0))))


def _row_rsqrt(ss_ref, k):
    return lax.rsqrt(ss_ref[...] * (1.0 / k) + EPS)


def _prenorm_specs(tm, k):
    return [pl.BlockSpec((tm, k), lambda j, i: (i, 0)), pl.BlockSpec((tm, 1), lambda j, i: (i, 0))]


def _mm_kernel(x_ref, ss_ref, w_hbm, o_ref, w_buf, sems, *, gelu, layer, tn):
    slot = _prefetch_weight_block([(w_hbm, w_buf)], sems, layer, 0, tn)
    acc = jnp.dot(x_ref[...], w_buf[slot].astype(BF16), preferred_element_type=F32)
    acc = acc * _row_rsqrt(ss_ref, x_ref.shape[1])
    if gelu:
        acc = _gelu_exact(acc)
    o_ref[...] = acc.astype(o_ref.dtype)


def _weight_spec(k, tn, layer):
    return pl.BlockSpec((None, k, tn), lambda i, j: (layer, 0, j))


def _prefetch_weight_block(weights, sems, layer, col_block0, tn):
    j, i = pl.program_id(0), pl.program_id(1)
    slot = j % 2

    def copies(block, into):
        cols = pl.ds(pl.multiple_of((col_block0 + block) * tn, tn), tn)
        return [pltpu.make_async_copy(hbm.at[layer, :, cols], buf.at[into], sems.at[n, into])
                for n, (hbm, buf) in enumerate(weights)]

    @pl.when(i == 0)
    def _():
        @pl.when(j == 0)
        def _():
            for copy in copies(0, 0):
                copy.start()

        for copy in copies(j, slot):
            copy.wait()

        @pl.when(j + 1 < pl.num_programs(0))
        def _():
            for copy in copies(j + 1, 1 - slot):
                copy.start(priority=1)

    return slot


def _weight_prefetch_scratch(n_weights, k, tn):
    return [pltpu.VMEM((2, k, tn), F32)] * n_weights + [pltpu.SemaphoreType.DMA((n_weights, 2))]


def matmul(x, ss, w, layer, out_dtype, *, gelu=False, tm=512, tn=1024):
    m, k = x.shape
    n = w.shape[2]
    tm, tn = _tile(m, tm), _tile(n, tn)
    return pl.pallas_call(
        functools.partial(_mm_kernel, gelu=gelu, layer=layer, tn=tn),
        grid=(n // tn, m // tm),
        in_specs=_prenorm_specs(tm, k) + [pl.BlockSpec(memory_space=pl.ANY)],
        out_specs=pl.BlockSpec((tm, tn), lambda j, i: (i, j)),
        out_shape=jax.ShapeDtypeStruct((m, n), out_dtype),
        scratch_shapes=_weight_prefetch_scratch(1, k, tn),
        compiler_params=_params("arbitrary", "arbitrary"),
        name="matmul_gelu" if gelu else "matmul",
    )(x, ss, w)


def _mm_by_residue_kernel(x_ref, ss_ref, w_hbm, o_ref, acc_ref, w_buf, sems, *, dilation, layer, col_block0, tn):
    slot = _prefetch_weight_block([(w_hbm, w_buf)], sems, layer, col_block0, tn)
    acc = jnp.dot(x_ref[...], w_buf[slot].astype(BF16), preferred_element_type=F32)
    acc = acc * _row_rsqrt(ss_ref, x_ref.shape[1])
    rows = acc.shape[0] // dilation
    for c in range(acc_ref.shape[0]):
        lanes = slice(c * LANES, (c + 1) * LANES)
        acc_ref[c] = acc[:, lanes]
        for r in range(dilation):
            o_ref[r, :, lanes] = acc_ref[c, pl.ds(r, rows, stride=dilation), :].astype(o_ref.dtype)


def matmul_by_residue(x, ss, w, layer, group, n_groups, dilation, *, tm=512, tn=1024):
    m, k = x.shape
    n = w.shape[2] // n_groups
    tm, tn = _tile(m, tm), _tile(n, tn)
    return pl.pallas_call(
        functools.partial(_mm_by_residue_kernel, dilation=dilation, layer=layer, col_block0=group * (n // tn), tn=tn),
        grid=(n // tn, m // tm),
        in_specs=_prenorm_specs(tm, k) + [pl.BlockSpec(memory_space=pl.ANY)],
        out_specs=pl.BlockSpec((dilation, tm // dilation, tn), lambda j, i: (0, i, j)),
        out_shape=jax.ShapeDtypeStruct((dilation, m // dilation, n), BF16),
        scratch_shapes=[pltpu.VMEM((tn // LANES, tm, LANES), F32)] + _weight_prefetch_scratch(1, k, tn),
        compiler_params=_params("arbitrary", "arbitrary"),
        name=f"matmul_by_residue_d{dilation}",
    )(x, ss, w)


def _mm_residual_kernel(x_ref, w_ref, r_ref, gn_ref, o_ref, ob_ref, ss_ref):
    h = r_ref[...] + jnp.dot(x_ref[...], w_ref[...], preferred_element_type=F32)
    o_ref[...] = h
    ob_ref[...] = (h * gn_ref[...]).astype(BF16)
    part = jnp.sum(h * h, axis=-1, keepdims=True)

    @pl.when(pl.program_id(1) == 0)
    def _():
        ss_ref[...] = part

    @pl.when(pl.program_id(1) > 0)
    def _():
        ss_ref[...] += part


def matmul_residual(x, w, layer, res, next_gain):
    m, k = x.shape
    n = w.shape[2]
    tm, tn = (_tile(m, 1024), _tile(n, 512)) if k <= 4096 else (_tile(m, 512), _tile(n, 512))
    return pl.pallas_call(
        _mm_residual_kernel,
        grid=(m // tm, n // tn),
        in_specs=[
            pl.BlockSpec((tm, k), lambda i, j: (i, 0)),
            _weight_spec(k, tn, layer),
            pl.BlockSpec((tm, tn), lambda i, j: (i, j)),
            pl.BlockSpec((1, tn), lambda i, j: (0, j)),
        ],
        out_specs=[pl.BlockSpec((tm, tn), lambda i, j: (i, j)), pl.BlockSpec((tm, tn), lambda i, j: (i, j)),
                   pl.BlockSpec((tm, 1), lambda i, j: (i, 0))],
        out_shape=[jax.ShapeDtypeStruct((m, n), F32), jax.ShapeDtypeStruct((m, n), BF16),
                   jax.ShapeDtypeStruct((m, 1), F32)],
        compiler_params=_params("parallel", "arbitrary"),
        name="matmul_residual",
    )(x, w, res, next_gain.reshape(1, n))


def _gate_up_kernel(x_ref, ss_ref, wg_hbm, wu_hbm, o_ref, wg_buf, wu_buf, sems, *, layer, tn):
    slot = _prefetch_weight_block([(wg_hbm, wg_buf), (wu_hbm, wu_buf)], sems, layer, 0, tn)
    x = x_ref[...]
    r = _row_rsqrt(ss_ref, x_ref.shape[1])
    g = jnp.dot(x, wg_buf[slot].astype(BF16), preferred_element_type=F32) * r
    u = jnp.dot(x, wu_buf[slot].astype(BF16), preferred_element_type=F32) * r
    o_ref[...] = (g * jax.nn.sigmoid(g) * u).astype(o_ref.dtype)


def gate_up(x, ss, wg, wu, layer, *, tm=1024, tn=256):
    m, k = x.shape
    n = wg.shape[2]
    tm, tn = _tile(m, tm), _tile(n, tn)
    return pl.pallas_call(
        functools.partial(_gate_up_kernel, layer=layer, tn=tn),
        grid=(n // tn, m // tm),
        in_specs=_prenorm_specs(tm, k) + [pl.BlockSpec(memory_space=pl.ANY)] * 2,
        out_specs=pl.BlockSpec((tm, tn), lambda j, i: (i, j)),
        out_shape=jax.ShapeDtypeStruct((m, n), BF16),
        scratch_shapes=_weight_prefetch_scratch(2, k, tn),
        compiler_params=_params("arbitrary", "arbitrary"),
        name="gate_up",
    )(x, ss, wg, wu)


def _sgu_kernel(u_ref, v_ref, gv_ref, ws_ref, bs_ref, o_ref, *, n_groups, chunks):
    v = v_ref[...]
    ms = jnp.mean(v * v, axis=-1, keepdims=True)
    vn = (v * lax.rsqrt(ms + EPS) * gv_ref[...]).astype(BF16)
    row = lax.broadcasted_iota(jnp.int32, (SGU_CHUNK, SGU_CHUNK), 0)
    col = lax.broadcasted_iota(jnp.int32, (SGU_CHUNK, SGU_CHUNK), 1)
    bs = bs_ref[...]
    for g in range(n_groups):
        w = jnp.where(col <= row, ws_ref[g], 0.0).astype(BF16)
        b = bs[:, g : g + 1]
        lanes = slice(g * HEAD_DIM, (g + 1) * HEAD_DIM)
        for c in range(chunks):
            rows = slice(c * SGU_CHUNK, (c + 1) * SGU_CHUNK)
            mixed = jnp.dot(w, vn[rows, lanes], preferred_element_type=F32) + b
            o_ref[rows, lanes] = (u_ref[rows, lanes] * mixed).astype(o_ref.dtype)


def sgu_gate(z, g_v, w_s, b_s):
    s, two_d = z.shape
    d = two_d // 2
    n_groups = d // HEAD_DIM
    chunks = 2 if s % (2 * SGU_CHUNK) == 0 else 1
    tr = chunks * SGU_CHUNK
    return pl.pallas_call(
        functools.partial(_sgu_kernel, n_groups=n_groups, chunks=chunks),
        grid=(s // tr,),
        in_specs=[
            pl.BlockSpec((tr, d), lambda i: (i, 0)),
            pl.BlockSpec((tr, d), lambda i: (i, 1)),
            pl.BlockSpec((1, d), lambda i: (0, 0)),
            pl.BlockSpec((n_groups, SGU_CHUNK, SGU_CHUNK), lambda i: (0, 0, 0)),
            pl.BlockSpec((SGU_CHUNK, n_groups), lambda i: (0, 0)),
        ],
        out_specs=pl.BlockSpec((tr, d), lambda i: (i, 0)),
        out_shape=jax.ShapeDtypeStruct((s, d), BF16),
        compiler_params=_params("parallel"),
        name="sgu_gate",
    )(z, z, g_v.reshape(1, d), w_s, b_s.T)


def _sb_kernel(q_ref, k_ref, v_ref, o_ref, acc_ref, *, tb, heads, scale_log2):
    i = pl.program_id(1)
    row = lax.broadcasted_iota(jnp.int32, (tb, tb), 0)
    col = lax.broadcasted_iota(jnp.int32, (tb, tb), 1)
    neg_later = jnp.where(row > col, -1.0, 0.0).astype(BF16)
    causal = col < row
    dims = (((1,), (1,)), ((), ()))

    def block(j, carries, diagonal):
        start = pl.multiple_of(j * tb, tb)
        chains = range(heads)
        lanes = [slice(c * HEAD_DIM, (c + 1) * HEAD_DIM) for c in chains]
        z2 = [lax.dot_general(q_ref[:, lanes[c]], k_ref[pl.ds(start, tb), lanes[c]], dims,
                              preferred_element_type=F32) * scale_log2 for c in chains]
        log2_beta, parts, new_carries = [], [], []
        for c in chains:
            sp2 = jnp.maximum(z2[c], 0.0) + jnp.log(1.0 + jnp.exp2(-jnp.abs(z2[c]))) * LOG2_E
            log2_beta.append(z2[c] - sp2)
            if diagonal:
                sp2 = jnp.where(causal, sp2, 0.0)
            hi = pltpu.bitcast(pltpu.bitcast(sp2, jnp.uint32) & jnp.uint32(0xFFFF0000), F32)
            parts.append(jnp.concatenate([hi.astype(BF16), (sp2 - hi).astype(BF16)], axis=0))
            new_carries.append(carries[c] - jnp.sum(sp2, axis=-1, keepdims=True))
        both = [jnp.dot(parts[c], neg_later, preferred_element_type=F32) for c in chains]
        a = []
        for c in chains:
            log2_tail = both[c][:tb] + both[c][tb:] + carries[c]
            a_c = jnp.exp2(log2_beta[c] + log2_tail)
            if diagonal:
                a_c = jnp.where(causal, a_c, 0.0)
            a.append(a_c.astype(BF16))
        for c in chains:
            av = jnp.dot(a[c], v_ref[pl.ds(start, tb), lanes[c]], preferred_element_type=F32)
            if diagonal:
                acc_ref[c] = av
            else:
                acc_ref[c] += av
        return tuple(new_carries)

    def any_live(carries):
        return functools.reduce(jnp.maximum, [jnp.max(c) for c in carries]) > EXP2_TO_ZERO

    def body(state):
        it, carries, _ = state
        carries = block(i - 1 - it, carries, False)
        return it + 1, carries, any_live(carries)

    carries = block(i, tuple(jnp.zeros((tb, 1), F32) for _ in range(heads)), True)
    lax.while_loop(lambda state: jnp.logical_and(state[0] < i, state[2]), body,
                   (jnp.int32(0), carries, any_live(carries)))
    for c in range(heads):
        o_ref[:, c * HEAD_DIM : (c + 1) * HEAD_DIM] = acc_ref[c].astype(o_ref.dtype)


def stick_breaking(qkv, n_heads):
    s = qkv.shape[0]
    tb = _tile(s, 256)
    heads = _tile(n_heads, 4)
    groups = n_heads // heads
    width = heads * HEAD_DIM
    return pl.pallas_call(
        functools.partial(_sb_kernel, tb=tb, heads=heads, scale_log2=LOG2_E / math.sqrt(HEAD_DIM)),
        grid=(groups, s // tb),
        in_specs=[
            pl.BlockSpec((tb, width), lambda h, i: (i, h)),
            pl.BlockSpec((s, width), lambda h, i: (0, groups + h)),
            pl.BlockSpec((s, width), lambda h, i: (0, 2 * groups + h)),
        ],
        out_specs=pl.BlockSpec((tb, width), lambda h, i: (i, h)),
        out_shape=jax.ShapeDtypeStruct((s, n_heads * HEAD_DIM), BF16),
        scratch_shapes=[pltpu.VMEM((heads, tb, HEAD_DIM), F32)],
        compiler_params=_params("parallel", "arbitrary"),
        name="stick_breaking",
    )(qkv, qkv, qkv)


def _dilated_kernel(q_ref, kp_ref, kc_ref, vp_ref, vc_ref, o_ref, lse_ref, *, n_heads, band, scale):
    has_prev = pl.program_id(1) > 0
    qi = lax.broadcasted_iota(jnp.int32, (band, band), 0)
    kj = lax.broadcasted_iota(jnp.int32, (band, band), 1)
    valid_prev = (kj >= qi) & has_prev
    valid_cur = kj <= qi
    lane = lax.broadcasted_iota(jnp.int32, (band, LANES), 1)
    dims = (((1,), (1,)), ((), ()))
    lse_all = jnp.zeros((band, LANES), F32)
    chunk = _tile(n_heads, 4)
    for h0 in range(0, n_heads, chunk):
        heads = range(h0, h0 + chunk)
        lanes = {h: slice(h * HEAD_DIM, (h + 1) * HEAD_DIM) for h in heads}
        zp = {h: lax.dot_general(q_ref[:, lanes[h]], kp_ref[:, lanes[h]], dims, preferred_element_type=F32)
              for h in heads}
        zc = {h: lax.dot_general(q_ref[:, lanes[h]], kc_ref[:, lanes[h]], dims, preferred_element_type=F32)
              for h in heads}
        wp, wc = {}, {}
        for h in heads:
            zp_h = jnp.where(valid_prev, zp[h] * scale, -jnp.inf)
            zc_h = jnp.where(valid_cur, zc[h] * scale, -jnp.inf)
            m = jnp.max(jnp.maximum(zp_h, zc_h), axis=-1, keepdims=True)
            pp = jnp.exp(zp_h - m)
            pc = jnp.exp(zc_h - m)
            den = jnp.sum(pp + pc, axis=-1, keepdims=True)
            inv = 1.0 / den
            wp[h] = (pp * inv).astype(BF16)
            wc[h] = (pc * inv).astype(BF16)
            lse_all = jnp.where(lane == h, m + jnp.log(den), lse_all)
        for h in heads:
            o_ref[:, lanes[h]] = (jnp.dot(wp[h], vp_ref[:, lanes[h]], preferred_element_type=F32)
                                  + jnp.dot(wc[h], vc_ref[:, lanes[h]], preferred_element_type=F32))
    lse_ref[...] = lse_all


def dilated_group(qkv, n_heads, band):
    dilation, n, _ = qkv.shape
    assert n_heads <= LANES
    width = n_heads * HEAD_DIM
    blk = (None, band, width)
    cur = lambda which: pl.BlockSpec(blk, lambda r, b: (r, b, which))
    prev = lambda which: pl.BlockSpec(blk, lambda r, b: (r, jnp.maximum(b - 1, 0), which))
    return pl.pallas_call(
        functools.partial(_dilated_kernel, n_heads=n_heads, band=band, scale=1.0 / math.sqrt(HEAD_DIM)),
        grid=(dilation, n // band),
        in_specs=[cur(0), prev(1), cur(1), prev(2), cur(2)],
        out_specs=[
            pl.BlockSpec(blk, lambda r, b: (r, b, 0)),
            pl.BlockSpec((None, band, LANES), lambda r, b: (r, b, 0)),
        ],
        out_shape=[
            jax.ShapeDtypeStruct((dilation, n, width), F32),
            jax.ShapeDtypeStruct((dilation, n, LANES), F32),
        ],
        compiler_params=_params("parallel", "arbitrary"),
        name=f"dilated_d{dilation}",
    )(qkv, qkv, qkv, qkv, qkv)


def _mix_kernel(*refs, dilations, n_heads):
    n_groups = len(dilations)
    o_refs, lse_refs = refs[:n_groups], refs[n_groups : 2 * n_groups]
    out_ref, o_seq, lse_seq = refs[2 * n_groups :]
    tr = out_ref.shape[0]
    for g, d in enumerate(dilations):
        for r in range(d):
            rows = pl.ds(r, tr // d, stride=d)
            lse_seq[g, rows, :] = lse_refs[g][r]
            for h in range(n_heads):
                o_seq[g, h, rows, :] = o_refs[g][r, :, h * HEAD_DIM : (h + 1) * HEAD_DIM]
    lses = [lse_seq[g] for g in range(n_groups)]
    m = functools.reduce(jnp.maximum, lses)
    es = [jnp.exp(l - m) for l in lses]
    total = functools.reduce(jnp.add, es)
    alphas = [e / total for e in es]
    for h in range(n_heads):
        acc = alphas[0][:, h : h + 1] * o_seq[0, h]
        for g in range(1, n_groups):
            acc = acc + alphas[g][:, h : h + 1] * o_seq[g, h]
        out_ref[:, h * HEAD_DIM : (h + 1) * HEAD_DIM] = acc.astype(out_ref.dtype)


def mix_groups(outs, lses):
    n_groups = len(outs)
    dilations = tuple(o.shape[0] for o in outs)
    width = outs[0].shape[2]
    n_heads = width // HEAD_DIM
    s = outs[0].shape[0] * outs[0].shape[1]
    tr = _tile(s, 256)
    return pl.pallas_call(
        functools.partial(_mix_kernel, dilations=dilations, n_heads=n_heads),
        grid=(s // tr,),
        in_specs=[pl.BlockSpec((d, tr // d, width), lambda i: (0, i, 0)) for d in dilations]
        + [pl.BlockSpec((d, tr // d, LANES), lambda i: (0, i, 0)) for d in dilations],
        out_specs=pl.BlockSpec((tr, width), lambda i: (i, 0)),
        out_shape=jax.ShapeDtypeStruct((s, width), BF16),
        scratch_shapes=[pltpu.VMEM((n_groups, n_heads, tr, HEAD_DIM), F32), pltpu.VMEM((n_groups, tr, LANES), F32)],
        compiler_params=_params("parallel"),
        name="mix_groups",
    )(*outs, *lses)


def kernel(x, norm_mix, norm_ffn, norm_final, a_w_in, a_g_v, a_w_s, a_b_s, a_w_out,
           b_w_qkv, b_w_o, c_w_qkv, c_w_o, w_gate, w_up, w_down):
    batch, seq, d_model = x.shape
    depth = norm_mix.shape[0]
    n_dil = len(DIL_PAIRS)
    a_w_out, b_w_o, c_w_o, w_down = (w.astype(BF16) for w in (a_w_out, b_w_o, c_w_o, w_down))
    outs = []
    for b in range(batch):
        h = x[b]
        hb, ss = scale_with_row_stats(h, norm_mix[0])
        for i in range(depth):
            kind, j = i % N_MIXERS, i // N_MIXERS
            if kind == 0:
                z = matmul(hb, ss, a_w_in, j, F32, gelu=True)
                gated = sgu_gate(z, a_g_v[j], a_w_s[j], a_b_s[j])
                h, hb, ss = matmul_residual(gated, a_w_out, j, h, norm_ffn[i])
            elif kind == 1:
                n_heads = b_w_o.shape[1] // HEAD_DIM
                qkv = matmul(hb, ss, b_w_qkv, j, BF16)
                o = stick_breaking(qkv, n_heads)
                h, hb, ss = matmul_residual(o, b_w_o, j, h, norm_ffn[i])
            else:
                n_heads = c_w_o.shape[1] // HEAD_DIM
                parts = []
                for g, (window, dilation) in enumerate(DIL_PAIRS):
                    qkv = matmul_by_residue(hb, ss, c_w_qkv, j, g, n_dil, dilation)
                    parts.append(dilated_group(qkv, n_heads, window // dilation))
                o = mix_groups([p[0] for p in parts], [p[1] for p in parts])
                h, hb, ss = matmul_residual(o, c_w_o, j, h, norm_ffn[i])
            act = gate_up(hb, ss, w_gate, w_up, i)
            next_gain = norm_mix[i + 1] if i + 1 < depth else norm_final
            h, hb, ss = matmul_residual(act, w_down, i, h, next_gain)
        outs.append(rmsnorm(h, norm_final, F32))
    return jnp.stack(outs, axis=0)
```
